```python
import jax, jax.numpy as jnp
from jax import lax
import numpy as np

D_MODEL = 1024
BATCH = 4
SEQ = 8192
DEPTH = 2

LRU_WIDTH = 5 * D_MODEL // 4
LRU_BLOCKS = 16
LRU_BLOCK = LRU_WIDTH // LRU_BLOCKS
CONV_WIDTH = 4
LRU_C = 8.0
RW_HEAD = 64
RW_WIDTH = D_MODEL
RW_HEADS = RW_WIDTH // RW_HEAD
RW_DECAY_LORA = 64
RW_AAA_LORA = 64
RW_MV_LORA = 32
RW_GATE_LORA = 160
RW_LN_EPS = 64e-5
ATT_HEAD = 64
ATT_HEADS = D_MODEL // ATT_HEAD
ATT_KV_HEADS = ATT_HEADS // 4
ATT_GROUP = ATT_HEADS // ATT_KV_HEADS
ATT_WIDTH = ATT_HEADS * ATT_HEAD
KV_WIDTH = ATT_KV_HEADS * ATT_HEAD
WINDOW = 128
BLOCK = 128
D_FF = -(-8 * D_MODEL // (3 * 256)) * 256
N_BRANCH = 3
RMS_EPS = 1e-6
QK_EPS = 1e-6
NEG_INF = -1e30

RW_SIZES = (RW_WIDTH, RW_WIDTH, RW_WIDTH, RW_DECAY_LORA, RW_AAA_LORA, RW_GATE_LORA)
RW_COLS = sum(RW_SIZES)
RW_POINTS = tuple(int(v) for v in np.cumsum(RW_SIZES)[:-1])
IN_SIZES = (LRU_WIDTH, LRU_WIDTH, RW_COLS, ATT_WIDTH, KV_WIDTH, KV_WIDTH, N_BRANCH * D_MODEL)
IN_COLS = sum(IN_SIZES)
IN_POINTS = tuple(int(v) for v in np.cumsum(IN_SIZES)[:-1])

kernel_name = "hybrid_rglru_rwkv7_swa_parallel"


def rms_norm(x):
    xf = x.astype(jnp.float32)
    return (xf * lax.rsqrt(jnp.mean(xf * xf, -1, keepdims=True) + RMS_EPS)).astype(x.dtype)


def shift_right(t):
    return jnp.concatenate([jnp.zeros_like(t[:, :1]), t[:, :-1]], axis=1)


def causal_conv(x, w, b):
    S = x.shape[1]
    K = w.shape[0]
    xp = jnp.pad(x, ((0, 0), (K - 1, 0), (0, 0)))
    out = b
    for j in range(K):
        out = out + xp[:, j:j + S] * w[j]
    return out


def rg_lru(x, w_a, b_a, w_x, b_x, lam):
    B, S, _ = x.shape
    xb = x.reshape(B, S, LRU_BLOCKS, LRU_BLOCK)
    gate_a = jax.nn.sigmoid(jnp.einsum('bsni,nij->bsnj', xb, w_a).reshape(B, S, LRU_WIDTH) + b_a)
    gate_x = jax.nn.sigmoid(jnp.einsum('bsni,nij->bsnj', xb, w_x).reshape(B, S, LRU_WIDTH) + b_x)
    log_a = -LRU_C * gate_a.astype(jnp.float32) * jax.nn.softplus(-lam.astype(jnp.float32))
    a = jnp.exp(log_a)
    mult = jnp.sqrt(1.0 - jnp.exp(2.0 * log_a))
    u = x.astype(jnp.float32) * gate_x.astype(jnp.float32) * mult

    def combine(left, right):
        a_l, b_l = left
        a_r, b_r = right
        return a_l * a_r, a_r * b_l + b_r

    _, h = lax.associative_scan(combine, (a, u), axis=1)
    return h.astype(x.dtype)


def rwkv7_scan(r, w, k, v, a, b):
    B, S, H, N = r.shape

    def step(state, inp):
        r_t, w_t, k_t, v_t, a_t, b_t = inp
        sa = jnp.einsum('bhvk,bhk->bhv', state, a_t)
        state = (state * w_t[:, :, None, :] + sa[..., None] * b_t[:, :, None, :]
                 + v_t[..., None] * k_t[:, :, None, :])
        return state, jnp.einsum('bhvk,bhk->bhv', state, r_t)

    xs = tuple(jnp.moveaxis(t, 1, 0) for t in (r, w, k, v, a, b))
    init = jnp.zeros((B, H, N, N), jnp.float32)
    _, ys = lax.scan(step, init, xs)
    return jnp.moveaxis(ys, 0, 1)


def rwkv7_time_mix(p_rw, mu, w0, w2, a0, a2, g2, k_k, k_a, r_k, ln_w, ln_b, v_first, v_mix):
    B, S, _ = p_rw.shape
    p = p_rw + (shift_right(p_rw) - p_rw) * mu
    r, k, v, wl, al, gl = jnp.split(p, RW_POINTS, axis=-1)
    w = -jax.nn.softplus(-(w0 + jnp.tanh(wl) @ w2)) - 0.5
    a = jax.nn.sigmoid(a0 + al @ a2)
    g = jax.nn.sigmoid(gl) @ g2
    if v_mix is not None:
        v = v + (v_first - v) * v_mix

    def heads(t):
        return t.reshape(B, S, RW_HEADS, RW_HEAD).astype(jnp.float32)

    kk = heads(k * k_k)
    kk = kk * lax.rsqrt(jnp.maximum(jnp.sum(kk * kk, -1, keepdims=True), 1e-24))
    k = k * (1 + (a - 1) * k_a)
    rh, kh, vh, ah = heads(r), heads(k), heads(v), heads(a)
    decay = jnp.exp(-jnp.exp(heads(w)))
    y = rwkv7_scan(rh, decay, kh, vh, -kk, kk * ah)
    mean = jnp.mean(y, -1, keepdims=True)
    var = jnp.mean(jnp.square(y - mean), -1, keepdims=True)
    y = ((y - mean) * lax.rsqrt(var + RW_LN_EPS)).reshape(B, S, RW_WIDTH) * ln_w + ln_b
    r_k_h = r_k.reshape(RW_HEADS, RW_HEAD).astype(jnp.float32)
    bonus = jnp.sum(rh * kh * r_k_h, -1, keepdims=True) * vh
    out = (y + bonus.reshape(B, S, RW_WIDTH)) * g
    return out.astype(p_rw.dtype), v


def head_rms(t, gain):
    tf = t.astype(jnp.float32)
    return (tf * lax.rsqrt(jnp.mean(tf * tf, -1, keepdims=True) + QK_EPS) * gain).astype(t.dtype)


def band(t):
    B, S = t.shape[:2]
    tb = t.reshape((B, S // BLOCK, BLOCK) + t.shape[2:])
    prev = jnp.concatenate([jnp.zeros_like(tb[:, :1]), tb[:, :-1]], axis=1)
    return jnp.concatenate([prev, tb], axis=2)


def sliding_window_attention(q, k, v, q_gain, k_gain, sinks):
    B, S, _ = q.shape
    NB = S // BLOCK
    q = head_rms(q.reshape(B, S, ATT_KV_HEADS, ATT_GROUP, ATT_HEAD), q_gain)
    k = head_rms(k.reshape(B, S, ATT_KV_HEADS, ATT_HEAD), k_gain)
    v = v.reshape(B, S, ATT_KV_HEADS, ATT_HEAD)
    qb = q.reshape(B, NB, BLOCK, ATT_KV_HEADS, ATT_GROUP, ATT_HEAD)
    kb, vb = band(k), band(v)
    s = jnp.einsum('bnqkgd,bnjkd->bkgnqj', qb, kb).astype(jnp.float32) * (ATT_HEAD ** -0.5)
    qi = jnp.arange(BLOCK)[:, None] + BLOCK
    kj = jnp.arange(2 * BLOCK)[None, :]
    local = (kj <= qi) & (qi - kj < WINDOW)
    valid = (jnp.arange(NB)[:, None, None] * BLOCK - BLOCK + kj[None]) >= 0
    mask = local[None] & valid
    s = jnp.where(mask, s, NEG_INF)
    sink = sinks.reshape(ATT_KV_HEADS, ATT_GROUP).astype(jnp.float32)[None, :, :, None, None, None]
    m = jnp.maximum(jnp.max(s, -1, keepdims=True), sink)
    e = jnp.exp(s - m)
    probs = e / (jnp.sum(e, -1, keepdims=True) + jnp.exp(sink - m))
    o = jnp.einsum('bkgnqj,bnjkd->bnqkgd', probs.astype(v.dtype), vb)
    return o.reshape(B, S, ATT_WIDTH)


def swiglu(h, w_in, w_out):
    gate, up = jnp.split(h @ w_in, 2, axis=-1)
    return (jax.nn.silu(gate) * up) @ w_out


def setup_inputs(seed: int = 0) -> dict:
    key = jax.random.key(seed)
    ks = jax.random.split(key, 40)
    f32 = jnp.float32
    L = DEPTH

    def nrm(k, shape, scale):
        return jax.random.normal(k, shape, f32) * scale

    a_init = jax.random.uniform(ks[11], (L, LRU_WIDTH), f32, 0.9, 0.999)
    s_init = a_init ** (1.0 / LRU_C)
    lam = jnp.log(s_init) - jnp.log1p(-s_init)
    return {
        "x": nrm(ks[0], (BATCH, SEQ, D_MODEL), 1.0),
        "c": nrm(ks[1], (BATCH, D_MODEL), 1.0),
        "w_ada": nrm(ks[2], (L, D_MODEL, 6 * D_MODEL), 0.5 * D_MODEL ** -0.5),
        "b_ada": nrm(ks[3], (L, 6 * D_MODEL), 0.02),
        "w_in": nrm(ks[4], (L, D_MODEL, IN_COLS), D_MODEL ** -0.5),
        "conv_w": nrm(ks[5], (L, CONV_WIDTH, LRU_WIDTH), CONV_WIDTH ** -0.5),
        "conv_b": nrm(ks[6], (L, LRU_WIDTH), 0.02),
        "lru_wa": nrm(ks[7], (L, LRU_BLOCKS, LRU_BLOCK, LRU_BLOCK), LRU_BLOCK ** -0.5),
        "lru_ba": nrm(ks[8], (L, LRU_WIDTH), 0.02),
        "lru_wx": nrm(ks[9], (L, LRU_BLOCKS, LRU_BLOCK, LRU_BLOCK), LRU_BLOCK ** -0.5),
        "lru_bx": nrm(ks[10], (L, LRU_WIDTH), 0.02),
        "lru_lambda": lam,
        "w_lru_o": nrm(ks[12], (L, LRU_WIDTH, D_MODEL), LRU_WIDTH ** -0.5),
        "rw_mu": jax.random.uniform(ks[13], (L, RW_COLS), f32),
        "rw_w0": jax.random.uniform(ks[14], (L, RW_WIDTH), f32, -6.0, 0.0),
        "rw_w2": nrm(ks[15], (L, RW_DECAY_LORA, RW_WIDTH), 0.1 * RW_DECAY_LORA ** -0.5),
        "rw_a0": nrm(ks[16], (L, RW_WIDTH), 0.1),
        "rw_a2": nrm(ks[17], (L, RW_AAA_LORA, RW_WIDTH), 0.1 * RW_AAA_LORA ** -0.5),
        "rw_g2": nrm(ks[18], (L, RW_GATE_LORA, RW_WIDTH), RW_GATE_LORA ** -0.5),
        "rw_kk": 0.85 + nrm(ks[19], (L, RW_WIDTH), 0.05),
        "rw_ka": 1.0 + nrm(ks[20], (L, RW_WIDTH), 0.05),
        "rw_rk": nrm(ks[21], (L, RW_WIDTH), 0.1),
        "rw_ln_w": 1.0 + nrm(ks[22], (L, RW_WIDTH), 0.05),
        "rw_ln_b": nrm(ks[23], (L, RW_WIDTH), 0.02),
        "rw_v0": nrm(ks[24], (L - 1, RW_WIDTH), 0.1),
        "rw_v1": nrm(ks[25], (L - 1, D_MODEL, RW_MV_LORA), D_MODEL ** -0.5),
        "rw_v2": nrm(ks[26], (L - 1, RW_MV_LORA, RW_WIDTH), 0.1 * RW_MV_LORA ** -0.5),
        "w_rw_o": nrm(ks[27], (L, RW_WIDTH, D_MODEL), RW_WIDTH ** -0.5),
        "q_gain": 1.0 + nrm(ks[28], (L, ATT_HEAD), 0.05),
        "k_gain": 1.0 + nrm(ks[29], (L, ATT_HEAD), 0.05),
        "sinks": nrm(ks[30], (L, ATT_HEADS), 0.5),
        "w_att_o": nrm(ks[31], (L, ATT_WIDTH, D_MODEL), ATT_WIDTH ** -0.5),
        "w_out": nrm(ks[32], (L, D_MODEL, D_MODEL), D_MODEL ** -0.5),
        "w_ffn_in": nrm(ks[33], (L, D_MODEL, 2 * D_FF), D_MODEL ** -0.5),
        "w_ffn_out": nrm(ks[34], (L, D_FF, D_MODEL), D_FF ** -0.5),
    }


def reference(x, c, w_ada, b_ada, w_in, conv_w, conv_b, lru_wa, lru_ba, lru_wx, lru_bx,
              lru_lambda, w_lru_o, rw_mu, rw_w0, rw_w2, rw_a0, rw_a2, rw_g2, rw_kk, rw_ka,
              rw_rk, rw_ln_w, rw_ln_b, rw_v0, rw_v1, rw_v2, w_rw_o, q_gain, k_gain, sinks,
              w_att_o, w_out, w_ffn_in, w_ffn_out):
    c_act = jax.nn.silu(c)
    v_first = None
    for i in range(DEPTH):
        mod = (c_act @ w_ada[i] + b_ada[i])[:, None, :]
        sh1, sc1, g1, sh2, sc2, g2 = jnp.split(mod, 6, axis=-1)

        h = rms_norm(x) * (1 + sc1) + sh1
        proj = h @ w_in[i]
        lru_x, lru_y, p_rw, q, k, v, gates = jnp.split(proj, IN_POINTS, axis=-1)

        xa = causal_conv(lru_x, conv_w[i], conv_b[i])
        ha = rg_lru(xa, lru_wa[i], lru_ba[i], lru_wx[i], lru_bx[i], lru_lambda[i])
        o_a = (ha * jax.nn.gelu(lru_y)) @ w_lru_o[i]

        if i == 0:
            v_mix = None
        else:
            v_mix = jax.nn.sigmoid(rw_v0[i - 1] + (h @ rw_v1[i - 1]) @ rw_v2[i - 1])
        y_b, v_cur = rwkv7_time_mix(p_rw, rw_mu[i], rw_w0[i], rw_w2[i], rw_a0[i], rw_a2[i],
                                    rw_g2[i], rw_kk[i], rw_ka[i], rw_rk[i], rw_ln_w[i],
                                    rw_ln_b[i], v_first, v_mix)
        if i == 0:
            v_first = v_cur
        o_b = y_b @ w_rw_o[i]

        y_c = sliding_window_attention(q, k, v, q_gain[i], k_gain[i], sinks[i])
        o_c = y_c @ w_att_o[i]

        gate_a, gate_b, gate_c = jnp.split(jax.nn.sigmoid(gates), N_BRANCH, axis=-1)
        mixed = (gate_a * o_a + gate_b * o_b + gate_c * o_c) @ w_out[i]
        x = x + g1 * mixed

        h2 = rms_norm(x) * (1 + sc2) + sh2
        x = x + g2 * swiglu(h2, w_ffn_in[i], w_ffn_out[i])
    return x
```

```python
import functools

import jax
import jax.numpy as jnp
from jax import lax
from jax.experimental import pallas as pl
from jax.experimental.pallas import tpu as pltpu

D_MODEL = 1024
LRU_WIDTH = 1280
LRU_BLOCKS = 16
LRU_BLOCK = 80
LRU_SUPER = 640
CONV_WIDTH = 4
LRU_C = 8.0
RW_HEAD = 64
RW_WIDTH = 1024
RW_DECAY_LORA = 64
RW_AAA_LORA = 64
RW_MV_LORA = 32
RW_GATE_LORA = 160
RW_LN_EPS = 64e-5
ATT_HEAD = 64
ATT_HEADS = 16
ATT_KV_HEADS = 4
ATT_GROUP = 4
ATT_WIDTH = 1024
KV_WIDTH = 256
WINDOW = 128
D_FF = 2816
RMS_EPS = 1e-6
QK_EPS = 1e-6
NEG_INF = -1e30

LANES = 128
SUBLANES = 8
VMEM_LIMIT_BYTES = 56 * 1024 * 1024

RW_LORA_OFF = 3 * RW_WIDTH
RW_GATE_OFF = RW_LORA_OFF + LANES
RW_GATE_PAD = 2 * LANES
RW_COLS_PAD = RW_GATE_OFF + RW_GATE_PAD
RW_MV_PAD = LANES

RW_CHUNK = 64

F32 = jnp.float32
BF16 = jnp.bfloat16
HIGHEST = lax.Precision.HIGHEST


def _sigmoid(x):
    return 1.0 / (1.0 + jnp.exp(-x))


def _softplus(z):
    return jnp.maximum(z, 0.0) + jnp.log(1.0 + jnp.exp(-jnp.abs(z)))


def _params(sem):
    return pltpu.CompilerParams(dimension_semantics=sem, vmem_limit_bytes=VMEM_LIMIT_BYTES)


def _ada_kernel(c_ref, w_ref, b_ref, o_ref):
    c = c_ref[...]
    act = (c * _sigmoid(c)).astype(BF16)
    o_ref[0] = jnp.dot(act, w_ref[0].astype(BF16), preferred_element_type=F32) + b_ref[0]


def _ada(c, w_ada, b_ada):
    L, D, N = w_ada.shape
    B = c.shape[0]
    tn = D_MODEL
    return pl.pallas_call(
        _ada_kernel,
        grid=(L, N // tn),
        in_specs=[
            pl.BlockSpec((B, D), lambda l, j: (0, 0)),
            pl.BlockSpec((1, D, tn), lambda l, j: (l, 0, j)),
            pl.BlockSpec((1, 1, tn), lambda l, j: (l, 0, j)),
        ],
        out_specs=pl.BlockSpec((1, B, tn), lambda l, j: (l, 0, j)),
        out_shape=jax.ShapeDtypeStruct((L, B, N), F32),
        compiler_params=_params(("arbitrary", "arbitrary")),
        name="ada_mod",
    )(c, w_ada, b_ada.reshape(L, 1, N))


def _norm_mod_kernel(x_ref, sc_ref, sh_ref, o_ref):
    x = x_ref[0]
    ms = jnp.mean(x * x, axis=-1, keepdims=True)
    h = x * lax.rsqrt(ms + RMS_EPS) * (1.0 + sc_ref[0]) + sh_ref[0]
    o_ref[0] = h.astype(o_ref.dtype)


def _norm_mod(x, sc, sh, tm):
    B, S, D = x.shape
    vec = pl.BlockSpec((1, 1, D), lambda b, i: (b, 0, 0))
    return pl.pallas_call(
        _norm_mod_kernel,
        grid=(B, S // tm),
        in_specs=[pl.BlockSpec((1, tm, D), lambda b, i: (b, i, 0)), vec, vec],
        out_specs=pl.BlockSpec((1, tm, D), lambda b, i: (b, i, 0)),
        out_shape=jax.ShapeDtypeStruct((B, S, D), BF16),
        compiler_params=_params(("arbitrary", "arbitrary")),
        name="norm_mod",
    )(x, sc, sh)


def _mm_kernel(x_ref, w_ref, o_ref, *, n_chunk):
    x = x_ref[0]
    n = o_ref.shape[-1]
    for j in range(0, n, n_chunk):
        w = min(n_chunk, n - j)
        o_ref[0, :, j:j + w] = jnp.dot(x, w_ref[:, j:j + w], preferred_element_type=F32).astype(o_ref.dtype)


def _mm(x, w, tm, out_dtype, name):
    B, S, K = x.shape
    N = w.shape[1]
    return pl.pallas_call(
        functools.partial(_mm_kernel, n_chunk=512),
        grid=(B, S // tm),
        in_specs=[
            pl.BlockSpec((1, tm, K), lambda b, i: (b, i, 0)),
            pl.BlockSpec((K, N), lambda b, i: (0, 0)),
        ],
        out_specs=pl.BlockSpec((1, tm, N), lambda b, i: (b, i, 0)),
        out_shape=jax.ShapeDtypeStruct((B, S, N), out_dtype),
        compiler_params=_params(("arbitrary", "arbitrary")),
        name=name,
    )(x, w)


def _shift_rows(x, prev8, j, row8):
    xs = pltpu.roll(x, j, axis=0)
    ps = pltpu.roll(prev8, j, axis=0)
    head = jnp.where(row8 < j, ps, xs[:SUBLANES])
    return jnp.concatenate([head, xs[SUBLANES:]], axis=0)


def _lru_kernel(p_ref, cw_ref, cb_ref, wbd_ref, ba_ref, bx_ref, lam_ref, o_ref, prev_ref, h_ref, *, tm):
    @pl.when(pl.program_id(1) == 0)
    def _():
        prev_ref[...] = jnp.zeros_like(prev_ref)
        h_ref[...] = jnp.zeros_like(h_ref)

    x = p_ref[0, :, :LRU_WIDTH]
    y = p_ref[0, :, LRU_WIDTH:]
    prev8 = prev_ref[...]
    row8 = lax.broadcasted_iota(jnp.int32, (SUBLANES, LRU_WIDTH), 0)
    cw = cw_ref[...]
    xa = cb_ref[...] + x * cw[CONV_WIDTH - 1:CONV_WIDTH]
    for j in range(1, CONV_WIDTH):
        xa = xa + _shift_rows(x, prev8, j, row8) * cw[CONV_WIDTH - 1 - j:CONV_WIDTH - j]
    prev_ref[...] = x[tm - SUBLANES:]

    xb = xa.astype(BF16)
    pre = [jnp.dot(xb[:, s * LRU_SUPER:(s + 1) * LRU_SUPER], wbd_ref[s], preferred_element_type=F32)
           for s in range(LRU_WIDTH // LRU_SUPER)]
    pre_a = jnp.concatenate([p[:, :LRU_SUPER] for p in pre], axis=-1)
    pre_x = jnp.concatenate([p[:, LRU_SUPER:] for p in pre], axis=-1)
    gate_a = _sigmoid(pre_a + ba_ref[...])
    gate_x = _sigmoid(pre_x + bx_ref[...])
    log_a = -LRU_C * gate_a * _softplus(-lam_ref[...])
    a = jnp.exp(log_a)
    u = xa * gate_x * jnp.sqrt(1.0 - jnp.exp(2.0 * log_a))

    row = lax.broadcasted_iota(jnp.int32, (tm, LRU_WIDTH), 0)
    d = 1
    while d < tm:
        keep = row >= d
        u = jnp.where(keep, a * pltpu.roll(u, d, axis=0) + u, u)
        a = jnp.where(keep, a * pltpu.roll(a, d, axis=0), a)
        d *= 2
    h = a * h_ref[0:1, :] + u
    h_ref[...] = jnp.broadcast_to(h[tm - 1:tm, :], h_ref.shape)

    gelu = 0.5 * y * (1.0 + jnp.tanh(0.7978845608028654 * (y + 0.044715 * (y * y * y))))
    o_ref[0] = (h * gelu).astype(o_ref.dtype)


def _lru_branch(p_lru, cw, cb, wbd, ba, bx, lam, tm):
    B, S, _ = p_lru.shape
    row = lambda n: pl.BlockSpec((1, n), lambda b, i: (0, 0))
    return pl.pallas_call(
        functools.partial(_lru_kernel, tm=tm),
        grid=(B, S // tm),
        in_specs=[
            pl.BlockSpec((1, tm, 2 * LRU_WIDTH), lambda b, i: (b, i, 0)),
            pl.BlockSpec((CONV_WIDTH, LRU_WIDTH), lambda b, i: (0, 0)),
            row(LRU_WIDTH),
            pl.BlockSpec((LRU_WIDTH // LRU_SUPER, LRU_SUPER, 2 * LRU_SUPER), lambda b, i: (0, 0, 0)),
            row(LRU_WIDTH), row(LRU_WIDTH), row(LRU_WIDTH),
        ],
        out_specs=pl.BlockSpec((1, tm, LRU_WIDTH), lambda b, i: (b, i, 0)),
        out_shape=jax.ShapeDtypeStruct((B, S, LRU_WIDTH), BF16),
        scratch_shapes=[pltpu.VMEM((SUBLANES, LRU_WIDTH), F32), pltpu.VMEM((SUBLANES, LRU_WIDTH), F32)],
        compiler_params=_params(("arbitrary", "arbitrary")),
        name="lru_branch",
    )(p_lru, cw, cb.reshape(1, -1), wbd, ba.reshape(1, -1), bx.reshape(1, -1), lam.reshape(1, -1))


def _rw_prep_kernel(*refs, tm, has_mix):
    if has_mix:
        (p_ref, mu_ref, w0_ref, a0_ref, w2_ref, a2_ref, g2_ref, vf_ref, v0_ref, v2_ref,
         r_ref, lw_ref, k_ref, v_ref, a_ref, g_ref, prev_ref) = refs
    else:
        (p_ref, mu_ref, w0_ref, a0_ref, w2_ref, a2_ref, g2_ref,
         r_ref, lw_ref, k_ref, v_ref, a_ref, g_ref, prev_ref) = refs

    @pl.when(pl.program_id(1) == 0)
    def _():
        prev_ref[...] = jnp.zeros_like(prev_ref)

    x = p_ref[0, :, :RW_COLS_PAD]
    row8 = lax.broadcasted_iota(jnp.int32, (SUBLANES, RW_COLS_PAD), 0)
    xs = _shift_rows(x, prev_ref[...], 1, row8)
    prev_ref[...] = x[tm - SUBLANES:]
    p = x + (xs - x) * mu_ref[...]

    r_ref[0] = p[:, 0:RW_WIDTH]
    k_ref[0] = p[:, RW_WIDTH:2 * RW_WIDTH]
    v = p[:, 2 * RW_WIDTH:3 * RW_WIDTH]
    lora = p[:, RW_LORA_OFF:RW_GATE_OFF]
    gl = p[:, RW_GATE_OFF:RW_COLS_PAD]

    w = w0_ref[...] + jnp.dot(jnp.tanh(lora).astype(BF16), w2_ref[...], preferred_element_type=F32)
    w = -_softplus(-w) - 0.5
    lw_ref[0] = -jnp.exp(w)
    a_ref[0] = _sigmoid(a0_ref[...] + jnp.dot(lora.astype(BF16), a2_ref[...], preferred_element_type=F32))
    g_ref[0] = jnp.dot(_sigmoid(gl).astype(BF16), g2_ref[...], preferred_element_type=F32)
    if has_mix:
        hv = p_ref[0, :, RW_COLS_PAD:RW_COLS_PAD + RW_MV_PAD]
        mix = _sigmoid(v0_ref[...] + jnp.dot(hv.astype(BF16), v2_ref[...], preferred_element_type=F32))
        v = v + (vf_ref[0] - v) * mix
    v_ref[0] = v


def _rw_prep(p_rw, mu, w0, a0, w2p, a2p, g2p, tm, mix=None):
    B, S, ncol = p_rw.shape
    has_mix = mix is not None
    const = lambda shape: pl.BlockSpec(shape, lambda b, i: (0,) * len(shape))
    tile = lambda n: pl.BlockSpec((1, tm, n), lambda b, i: (b, i, 0))
    in_specs = [tile(ncol), const((1, RW_COLS_PAD)), const((1, RW_WIDTH)), const((1, RW_WIDTH)),
                const((LANES, RW_WIDTH)), const((LANES, RW_WIDTH)), const((RW_GATE_PAD, RW_WIDTH))]
    args = [p_rw, mu, w0, a0, w2p, a2p, g2p]
    if has_mix:
        v_first, v0, v2p = mix
        in_specs += [tile(RW_WIDTH), const((1, RW_WIDTH)), const((RW_MV_PAD, RW_WIDTH))]
        args += [v_first, v0, v2p]
    out = jax.ShapeDtypeStruct((B, S, RW_WIDTH), F32)
    return pl.pallas_call(
        functools.partial(_rw_prep_kernel, tm=tm, has_mix=has_mix),
        grid=(B, S // tm),
        in_specs=in_specs,
        out_specs=[tile(RW_WIDTH)] * 6,
        out_shape=[out] * 6,
        scratch_shapes=[pltpu.VMEM((SUBLANES, RW_COLS_PAD), F32)],
        compiler_params=_params(("arbitrary", "arbitrary")),
        name="rw_prep",
    )(*args)


def _dot_hi(a, b):
    return jnp.dot(a, b, precision=HIGHEST, preferred_element_type=F32)


def _dot_hi_nt(a, b):
    return lax.dot_general(a, b, (((1,), (1,)), ((), ())), precision=HIGHEST, preferred_element_type=F32)


def _rw_scan_kernel(r_ref, lw_ref, k_ref, v_ref, a_ref, g_ref, kk_ref, ka_ref, rk_ref, lnw_ref, lnb_ref,
                    o_ref, m_ref, *, ts):
    C = RW_CHUNK
    N = 2 * C

    @pl.when(pl.program_id(2) == 0)
    def _():
        m_ref[...] = jnp.zeros_like(m_ref)

    ri = lax.broadcasted_iota(jnp.int32, (N, N), 0)
    ci = lax.broadcasted_iota(jnp.int32, (N, N), 1)
    same = (ri >= C) == (ci >= C)
    strict = same & (ri > ci)
    incl = same & (ri >= ci)
    eye = (ri == ci).astype(F32)
    seg_sum = same.astype(F32)
    tri_bd = incl.astype(F32)
    lane_lo = lax.broadcasted_iota(jnp.int32, (C, N), 1) < RW_HEAD

    def stack_masked(x):
        return jnp.concatenate([jnp.where(lane_lo, x, 0.0), jnp.where(lane_lo, 0.0, x)], axis=0)

    def stack(x):
        return jnp.concatenate([x, x], axis=0)

    kk_p = kk_ref[...]
    ka_p = ka_ref[...]
    rk_p = rk_ref[...]
    lnw = lnw_ref[...]
    lnb = lnb_ref[...]

    def chunk(c, carry):
        sl = pl.ds(pl.multiple_of(c * C, C), C)
        r = r_ref[0, sl, :]
        lw = lw_ref[0, sl, :]
        k0 = k_ref[0, sl, :]
        v = v_ref[0, sl, :]
        a = a_ref[0, sl, :]
        g = g_ref[0, sl, :]

        kkr = k0 * kk_p
        kk = kkr * lax.rsqrt(jnp.maximum(_dot_hi(kkr * kkr, seg_sum), 1e-24))
        k = k0 * (1.0 + (a - 1.0) * ka_p)
        b_s = kk * a

        cum = _dot_hi(tri_bd, stack(lw))[:C]
        last = cum[C - 1:C, :]
        e_inv = jnp.exp(-cum)
        e_out = jnp.exp(last - cum)
        a_t = -kk * jnp.exp(cum - lw)
        r_t = r * jnp.exp(cum)

        a_st = stack_masked(a_t)
        r_st = stack_masked(r_t)
        v_st = stack_masked(v)
        bb = stack(b_s * e_inv)
        kb = stack(k * e_inv)
        ab = jnp.where(strict, _dot_hi_nt(a_st, bb), 0.0)
        ak = jnp.where(strict, _dot_hi_nt(a_st, kb), 0.0)
        rb = jnp.where(incl, _dot_hi_nt(r_st, bb), 0.0)
        rk = jnp.where(incl, _dot_hi_nt(r_st, kb), 0.0)

        t_inv = eye + ab
        pw = ab
        for _ in range(5):
            pw = _dot_hi(pw, pw)
            t_inv = t_inv + _dot_hi(t_inv, pw)

        u0 = _dot_hi(t_inv, _dot_hi(ak, v_st))
        w_st = _dot_hi(t_inv, a_st)
        y0 = _dot_hi(rk, v_st)

        m = m_ref[...]
        u = _dot_hi(w_st, m) + u0
        y_st = _dot_hi(r_st, m) + _dot_hi(rb, u) + y0
        decay_col = jnp.broadcast_to(jnp.exp(last), (N, N)).T
        bh_t = stack_masked(b_s * e_out).T
        kh_t = stack_masked(k * e_out).T
        m_ref[...] = decay_col * m + _dot_hi(bh_t, u) + _dot_hi(kh_t, v_st)

        y = y_st[:C] + y_st[C:]
        mean = _dot_hi(y, seg_sum) * (1.0 / RW_HEAD)
        yc = y - mean
        var = _dot_hi(yc * yc, seg_sum) * (1.0 / RW_HEAD)
        yn = yc * lax.rsqrt(var + RW_LN_EPS) * lnw + lnb
        bonus = _dot_hi(r * k * rk_p, seg_sum) * v
        o_ref[0, sl, :] = ((yn + bonus) * g).astype(o_ref.dtype)
        return carry

    lax.fori_loop(0, ts // C, chunk, 0)


def _rw_scan(r, lw, k, v, a, g, kk_p, ka_p, rk_p, lnw, lnb, ts):
    B, S, W = r.shape
    tile = pl.BlockSpec((1, ts, LANES), lambda b, h, i: (b, i, h))
    vec = pl.BlockSpec((1, LANES), lambda b, h, i: (0, h))
    return pl.pallas_call(
        functools.partial(_rw_scan_kernel, ts=ts),
        grid=(B, W // LANES, S // ts),
        in_specs=[tile] * 6 + [vec] * 5,
        out_specs=tile,
        out_shape=jax.ShapeDtypeStruct((B, S, W), BF16),
        scratch_shapes=[pltpu.VMEM((LANES, LANES), F32)],
        compiler_params=_params(("arbitrary", "arbitrary", "arbitrary")),
        name="rw_scan",
    )(r, lw, k, v, a, g, kk_p, ka_p, rk_p, lnw, lnb)


def _head_rms(t, gain):
    return t * lax.rsqrt(jnp.mean(t * t, axis=-1, keepdims=True) + QK_EPS) * gain


def _attn_kernel(q_ref, kc_ref, kp_ref, vc_ref, vp_ref, qg_ref, kg_ref, sink_ref, o_ref):
    n = pl.program_id(1)
    qi = lax.broadcasted_iota(jnp.int32, (WINDOW, 2 * WINDOW), 0) + WINDOW
    kj = lax.broadcasted_iota(jnp.int32, (WINDOW, 2 * WINDOW), 1)
    mask = (kj <= qi) & (qi - kj < WINDOW) & ((n * WINDOW - WINDOW + kj) >= 0)
    qg = qg_ref[...]
    kg = kg_ref[...]
    outs = []
    for j in range(ATT_KV_HEADS):
        cs = slice(j * ATT_HEAD, (j + 1) * ATT_HEAD)
        kb = jnp.concatenate([_head_rms(kp_ref[0, :, cs], kg), _head_rms(kc_ref[0, :, cs], kg)], axis=0)
        vb = jnp.concatenate([vp_ref[0, :, cs], vc_ref[0, :, cs]], axis=0).astype(BF16)
        kb = kb.astype(BF16)
        for gi in range(ATT_GROUP):
            hd = j * ATT_GROUP + gi
            q = _head_rms(q_ref[0, :, hd * ATT_HEAD:(hd + 1) * ATT_HEAD], qg).astype(BF16)
            s = lax.dot_general(q, kb, (((1,), (1,)), ((), ())), preferred_element_type=F32)
            s = jnp.where(mask, s * (ATT_HEAD ** -0.5), NEG_INF)
            sink = sink_ref[hd:hd + 1, 0:1]
            m = jnp.maximum(jnp.max(s, axis=-1, keepdims=True), sink)
            e = jnp.exp(s - m)
            probs = e / (jnp.sum(e, axis=-1, keepdims=True) + jnp.exp(sink - m))
            outs.append(jnp.dot(probs.astype(BF16), vb, preferred_element_type=F32))
    o_ref[0] = jnp.concatenate(outs, axis=-1).astype(o_ref.dtype)


def _attention(p_att, q_gain, k_gain, sinks_b):
    B, S, _ = p_att.shape
    nq = ATT_WIDTH // KV_WIDTH
    cur = lambda col: pl.BlockSpec((1, WINDOW, KV_WIDTH), lambda b, n: (b, n, col))
    prv = lambda col: pl.BlockSpec((1, WINDOW, KV_WIDTH), lambda b, n: (b, jnp.maximum(n - 1, 0), col))
    return pl.pallas_call(
        _attn_kernel,
        grid=(B, S // WINDOW),
        in_specs=[
            pl.BlockSpec((1, WINDOW, ATT_WIDTH), lambda b, n: (b, n, 0)),
            cur(nq), prv(nq), cur(nq + 1), prv(nq + 1),
            pl.BlockSpec((1, ATT_HEAD), lambda b, n: (0, 0)),
            pl.BlockSpec((1, ATT_HEAD), lambda b, n: (0, 0)),
            pl.BlockSpec((ATT_HEADS, LANES), lambda b, n: (0, 0)),
        ],
        out_specs=pl.BlockSpec((1, WINDOW, ATT_WIDTH), lambda b, n: (b, n, 0)),
        out_shape=jax.ShapeDtypeStruct((B, S, ATT_WIDTH), BF16),
        compiler_params=_params(("arbitrary", "arbitrary")),
        name="swa_attention",
    )(p_att, p_att, p_att, p_att, p_att, q_gain.reshape(1, -1), k_gain.reshape(1, -1), sinks_b)


def _merge_kernel(ya_ref, yb_ref, yc_ref, gt_ref, x_ref, g1_ref, wa_ref, wb_ref, wc_ref, wo_ref, o_ref):
    gt = gt_ref[0]
    o_a = jnp.dot(ya_ref[0], wa_ref[...], preferred_element_type=F32)
    mixed = _sigmoid(gt[:, 0:D_MODEL]) * o_a
    o_b = jnp.dot(yb_ref[0], wb_ref[...], preferred_element_type=F32)
    mixed = mixed + _sigmoid(gt[:, D_MODEL:2 * D_MODEL]) * o_b
    o_c = jnp.dot(yc_ref[0], wc_ref[...], preferred_element_type=F32)
    mixed = mixed + _sigmoid(gt[:, 2 * D_MODEL:3 * D_MODEL]) * o_c
    out = jnp.dot(mixed.astype(BF16), wo_ref[...], preferred_element_type=F32)
    o_ref[0] = x_ref[0] + g1_ref[0] * out


def _merge(ya, yb, yc, gates, x, g1, wa, wb, wc, wo, tm):
    B, S, D = x.shape
    tile = lambda n: pl.BlockSpec((1, tm, n), lambda b, i: (b, i, 0))
    const = lambda a: pl.BlockSpec(a.shape, lambda b, i: (0, 0))
    return pl.pallas_call(
        _merge_kernel,
        grid=(B, S // tm),
        in_specs=[tile(LRU_WIDTH), tile(RW_WIDTH), tile(ATT_WIDTH), tile(3 * D), tile(D),
                  pl.BlockSpec((1, 1, D), lambda b, i: (b, 0, 0)),
                  const(wa), const(wb), const(wc), const(wo)],
        out_specs=tile(D),
        out_shape=jax.ShapeDtypeStruct((B, S, D), F32),
        compiler_params=_params(("arbitrary", "arbitrary")),
        name="merge",
    )(ya, yb, yc, gates, x, g1, wa, wb, wc, wo)


FFN_CHUNKS = ((0, 1024), (1024, 1024), (2048, 768))


def _ffn_kernel(x_ref, sc_ref, sh_ref, g2_ref, wi_ref, wo_ref, o_ref):
    x = x_ref[0]
    ms = jnp.mean(x * x, axis=-1, keepdims=True)
    h = (x * lax.rsqrt(ms + RMS_EPS) * (1.0 + sc_ref[0]) + sh_ref[0]).astype(BF16)
    acc = jnp.zeros(x.shape, F32)
    for off, width in FFN_CHUNKS:
        gate = jnp.dot(h, wi_ref[:, off:off + width], preferred_element_type=F32)
        up = jnp.dot(h, wi_ref[:, D_FF + off:D_FF + off + width], preferred_element_type=F32)
        act = (gate * _sigmoid(gate) * up).astype(BF16)
        acc = acc + jnp.dot(act, wo_ref[off:off + width, :], preferred_element_type=F32)
    o_ref[0] = x + g2_ref[0] * acc


def _ffn(x, sc, sh, g2, wi, wo, tm):
    B, S, D = x.shape
    tile = pl.BlockSpec((1, tm, D), lambda b, i: (b, i, 0))
    vec = pl.BlockSpec((1, 1, D), lambda b, i: (b, 0, 0))
    return pl.pallas_call(
        _ffn_kernel,
        grid=(B, S // tm),
        in_specs=[tile, vec, vec, vec,
                  pl.BlockSpec(wi.shape, lambda b, i: (0, 0)),
                  pl.BlockSpec(wo.shape, lambda b, i: (0, 0))],
        out_specs=tile,
        out_shape=jax.ShapeDtypeStruct((B, S, D), F32),
        compiler_params=_params(("arbitrary", "arbitrary")),
        name="ffn",
    )(x, sc, sh, g2, wi, wo)


def _pad_rows(w, rows):
    return jnp.pad(w, ((0, rows - w.shape[0]), (0, 0)))


def _pad_cols(w, cols):
    return jnp.pad(w, ((0, 0), (0, cols - w.shape[1])))


def _block_diag_gates(wa, wx):
    per = LRU_SUPER // LRU_BLOCK
    out = []
    for s in range(LRU_WIDTH // LRU_SUPER):
        halves = []
        for w in (wa, wx):
            m = jnp.zeros((LRU_SUPER, LRU_SUPER), w.dtype)
            for n in range(per):
                m = lax.dynamic_update_slice(m, w[s * per + n], (n * LRU_BLOCK, n * LRU_BLOCK))
            halves.append(m)
        out.append(jnp.concatenate(halves, axis=1))
    return jnp.stack(out).astype(BF16)


def _tile_rows(S, want):
    t = min(want, S)
    while S % t:
        t //= 2
    return t


def kernel(x, c, w_ada, b_ada, w_in, conv_w, conv_b, lru_wa, lru_ba, lru_wx, lru_bx, lru_lambda, w_lru_o, rw_mu, rw_w0, rw_w2, rw_a0, rw_a2, rw_g2, rw_kk, rw_ka, rw_rk, rw_ln_w, rw_ln_b, rw_v0, rw_v1, rw_v2, w_rw_o, q_gain, k_gain, sinks, w_att_o, w_out, w_ffn_in, w_ffn_out):
    B, S, D = x.shape
    depth = w_in.shape[0]
    assert D == D_MODEL and S % WINDOW == 0
    tm_mm = _tile_rows(S, 512)
    tm_seq = _tile_rows(S, 256)
    ts_scan = _tile_rows(S, 512)

    mod = _ada(c, w_ada, b_ada).reshape(depth, B, 6, 1, D)

    o_lru = 2 * LRU_WIDTH
    o_rw = o_lru + 3 * RW_WIDTH + RW_DECAY_LORA + RW_AAA_LORA + RW_GATE_LORA
    o_att = o_rw + ATT_WIDTH + 2 * KV_WIDTH

    v_first = None
    for i in range(depth):
        sh1, sc1, g1, sh2, sc2, g2 = (mod[i, :, j] for j in range(6))
        wi = w_in[i]
        w_lru = wi[:, :o_lru].astype(BF16)
        w_rw_cols = [_pad_cols(wi[:, o_lru:o_rw], RW_COLS_PAD)]
        if i > 0:
            w_rw_cols.append(_pad_cols(rw_v1[i - 1], RW_MV_PAD))
        w_rw = jnp.concatenate(w_rw_cols, axis=1).astype(BF16)
        w_att = wi[:, o_rw:o_att].astype(BF16)
        w_gate = wi[:, o_att:].astype(BF16)

        h = _norm_mod(x, sc1, sh1, tm_mm)
        p_lru = _mm(h, w_lru, tm_mm, F32, "proj_lru")
        p_rw = _mm(h, w_rw, tm_mm, F32, "proj_rw")
        p_att = _mm(h, w_att, tm_mm, F32, "proj_att")
        p_gate = _mm(h, w_gate, tm_mm, F32, "proj_gate")

        ya = _lru_branch(p_lru, conv_w[i], conv_b[i], _block_diag_gates(lru_wa[i], lru_wx[i]),
                         lru_ba[i], lru_bx[i], lru_lambda[i], tm_seq)

        mu = _pad_cols(rw_mu[i].reshape(1, -1), RW_COLS_PAD)
        w2p = _pad_rows(rw_w2[i], LANES).astype(BF16)
        a2p = jnp.concatenate([jnp.zeros_like(rw_a2[i]), rw_a2[i]], axis=0).astype(BF16)
        g2p = _pad_rows(rw_g2[i], RW_GATE_PAD).astype(BF16)
        mix = None
        if i > 0:
            mix = (v_first, rw_v0[i - 1].reshape(1, -1), _pad_rows(rw_v2[i - 1], RW_MV_PAD).astype(BF16))
        r, lw, k, v, a, g = _rw_prep(p_rw, mu, rw_w0[i].reshape(1, -1), rw_a0[i].reshape(1, -1),
                                     w2p, a2p, g2p, tm_seq, mix)
        if i == 0:
            v_first = v
        yb = _rw_scan(r, lw, k, v, a, g, rw_kk[i].reshape(1, -1), rw_ka[i].reshape(1, -1),
                      rw_rk[i].reshape(1, -1), rw_ln_w[i].reshape(1, -1), rw_ln_b[i].reshape(1, -1), ts_scan)

        sinks_b = jnp.broadcast_to(sinks[i].reshape(-1, 1), (ATT_HEADS, LANES))
        yc = _attention(p_att, q_gain[i], k_gain[i], sinks_b)

        x = _merge(ya, yb, yc, p_gate, x, g1, w_lru_o[i].astype(BF16), w_rw_o[i].astype(BF16),
                   w_att_o[i].astype(BF16), w_out[i].astype(BF16), tm_mm)
        x = _ffn(x, sc2, sh2, g2, w_ffn_in[i].astype(BF16), w_ffn_out[i].astype(BF16), tm_mm)
    return x
```

```python
import functools

import jax
import jax.numpy as jnp
from jax import lax
from jax.experimental import pallas as pl
from jax.experimental.pallas import tpu as pltpu

D_MODEL = 1024
LRU_WIDTH = 1280
LRU_BLOCKS = 16
LRU_BLOCK = 80
LRU_SUPER = 640
CONV_WIDTH = 4
LRU_C = 8.0
RW_HEAD = 64
RW_WIDTH = 1024
RW_DECAY_LORA = 64
RW_AAA_LORA = 64
RW_MV_LORA = 32
RW_GATE_LORA = 160
RW_LN_EPS = 64e-5
ATT_HEAD = 64
ATT_HEADS = 16
ATT_KV_HEADS = 4
ATT_GROUP = 4
ATT_WIDTH = 1024
KV_WIDTH = 256
WINDOW = 128
D_FF = 2816
RMS_EPS = 1e-6
QK_EPS = 1e-6
NEG_INF = -1e30

LANES = 128
SUBLANES = 8
VMEM_LIMIT_BYTES = 56 * 1024 * 1024

RW_LORA_OFF = 3 * RW_WIDTH
RW_GATE_OFF = RW_LORA_OFF + LANES
RW_GATE_PAD = 2 * LANES
RW_COLS_PAD = RW_GATE_OFF + RW_GATE_PAD
RW_MV_PAD = LANES

RW_CHUNK = 64

F32 = jnp.float32
BF16 = jnp.bfloat16
HIGHEST = lax.Precision.HIGHEST


def _sigmoid(x):
    return 1.0 / (1.0 + jnp.exp(-x))


def _softplus(z):
    return jnp.maximum(z, 0.0) + jnp.log(1.0 + jnp.exp(-jnp.abs(z)))


def _params(sem):
    return pltpu.CompilerParams(dimension_semantics=sem, vmem_limit_bytes=VMEM_LIMIT_BYTES)


def _ada_kernel(c_ref, w_ref, b_ref, o_ref):
    c = c_ref[...]
    act = (c * _sigmoid(c)).astype(BF16)
    o_ref[0] = jnp.dot(act, w_ref[0].astype(BF16), preferred_element_type=F32) + b_ref[0]


def _ada(c, w_ada, b_ada):
    L, D, N = w_ada.shape
    B = c.shape[0]
    tn = D_MODEL
    return pl.pallas_call(
        _ada_kernel,
        grid=(L, N // tn),
        in_specs=[
            pl.BlockSpec((B, D), lambda l, j: (0, 0)),
            pl.BlockSpec((1, D, tn), lambda l, j: (l, 0, j)),
            pl.BlockSpec((1, 1, tn), lambda l, j: (l, 0, j)),
        ],
        out_specs=pl.BlockSpec((1, B, tn), lambda l, j: (l, 0, j)),
        out_shape=jax.ShapeDtypeStruct((L, B, N), F32),
        compiler_params=_params(("arbitrary", "arbitrary")),
        name="ada_mod",
    )(c, w_ada, b_ada.reshape(L, 1, N))


def _norm_mod_kernel(x_ref, sc_ref, sh_ref, o_ref):
    x = x_ref[0]
    ms = jnp.mean(x * x, axis=-1, keepdims=True)
    h = x * lax.rsqrt(ms + RMS_EPS) * (1.0 + sc_ref[0]) + sh_ref[0]
    o_ref[0] = h.astype(o_ref.dtype)


def _norm_mod(x, sc, sh, tm):
    B, S, D = x.shape
    vec = pl.BlockSpec((1, 1, D), lambda b, i: (b, 0, 0))
    return pl.pallas_call(
        _norm_mod_kernel,
        grid=(B, S // tm),
        in_specs=[pl.BlockSpec((1, tm, D), lambda b, i: (b, i, 0)), vec, vec],
        out_specs=pl.BlockSpec((1, tm, D), lambda b, i: (b, i, 0)),
        out_shape=jax.ShapeDtypeStruct((B, S, D), BF16),
        compiler_params=_params(("arbitrary", "arbitrary")),
        name="norm_mod",
    )(x, sc, sh)


def _mm_kernel(x_ref, w_ref, o_ref, *, n_chunk):
    x = x_ref[0]
    n = o_ref.shape[-1]
    for j in range(0, n, n_chunk):
        w = min(n_chunk, n - j)
        o_ref[0, :, j:j + w] = jnp.dot(x, w_ref[:, j:j + w], preferred_element_type=F32).astype(o_ref.dtype)


def _mm(x, w, tm, out_dtype, name):
    B, S, K = x.shape
    N = w.shape[1]
    return pl.pallas_call(
        functools.partial(_mm_kernel, n_chunk=512),
        grid=(B, S // tm),
        in_specs=[
            pl.BlockSpec((1, tm, K), lambda b, i: (b, i, 0)),
            pl.BlockSpec((K, N), lambda b, i: (0, 0)),
        ],
        out_specs=pl.BlockSpec((1, tm, N), lambda b, i: (b, i, 0)),
        out_shape=jax.ShapeDtypeStruct((B, S, N), out_dtype),
        compiler_params=_params(("arbitrary", "arbitrary")),
        name=name,
    )(x, w)


def _shift_rows(x, prev8, j, row8):
    xs = pltpu.roll(x, j, axis=0)
    ps = pltpu.roll(prev8, j, axis=0)
    head = jnp.where(row8 < j, ps, xs[:SUBLANES])
    return jnp.concatenate([head, xs[SUBLANES:]], axis=0)


def _lru_kernel(p_ref, cw_ref, cb_ref, wbd_ref, ba_ref, bx_ref, lam_ref, o_ref, prev_ref, h_ref, *, tm):
    @pl.when(pl.program_id(1) == 0)
    def _():
        prev_ref[...] = jnp.zeros_like(prev_ref)
        h_ref[...] = jnp.zeros_like(h_ref)

    x = p_ref[0, :, :LRU_WIDTH]
    y = p_ref[0, :, LRU_WIDTH:]
    prev8 = prev_ref[...]
    row8 = lax.broadcasted_iota(jnp.int32, (SUBLANES, LRU_WIDTH), 0)
    cw = cw_ref[...]
    xa = cb_ref[...] + x * cw[CONV_WIDTH - 1:CONV_WIDTH]
    for j in range(1, CONV_WIDTH):
        xa = xa + _shift_rows(x, prev8, j, row8) * cw[CONV_WIDTH - 1 - j:CONV_WIDTH - j]
    prev_ref[...] = x[tm - SUBLANES:]

    xb = xa.astype(BF16)
    pre = [jnp.dot(xb[:, s * LRU_SUPER:(s + 1) * LRU_SUPER], wbd_ref[s], preferred_element_type=F32)
           for s in range(LRU_WIDTH // LRU_SUPER)]
    pre_a = jnp.concatenate([p[:, :LRU_SUPER] for p in pre], axis=-1)
    pre_x = jnp.concatenate([p[:, LRU_SUPER:] for p in pre], axis=-1)
    gate_a = _sigmoid(pre_a + ba_ref[...])
    gate_x = _sigmoid(pre_x + bx_ref[...])
    log_a = -LRU_C * gate_a * _softplus(-lam_ref[...])
    a = jnp.exp(log_a)
    u = xa * gate_x * jnp.sqrt(1.0 - jnp.exp(2.0 * log_a))

    row = lax.broadcasted_iota(jnp.int32, (tm, LRU_WIDTH), 0)
    d = 1
    while d < tm:
        keep = row >= d
        u = jnp.where(keep, a * pltpu.roll(u, d, axis=0) + u, u)
        a = jnp.where(keep, a * pltpu.roll(a, d, axis=0), a)
        d *= 2
    h = a * h_ref[0:1, :] + u
    h_ref[...] = jnp.broadcast_to(h[tm - 1:tm, :], h_ref.shape)

    gelu = 0.5 * y * (1.0 + jnp.tanh(0.7978845608028654 * (y + 0.044715 * (y * y * y))))
    o_ref[0] = (h * gelu).astype(o_ref.dtype)


def _lru_branch(p_lru, cw, cb, wbd, ba, bx, lam, tm):
    B, S, _ = p_lru.shape
    row = lambda n: pl.BlockSpec((1, n), lambda b, i: (0, 0))
    return pl.pallas_call(
        functools.partial(_lru_kernel, tm=tm),
        grid=(B, S // tm),
        in_specs=[
            pl.BlockSpec((1, tm, 2 * LRU_WIDTH), lambda b, i: (b, i, 0)),
            pl.BlockSpec((CONV_WIDTH, LRU_WIDTH), lambda b, i: (0, 0)),
            row(LRU_WIDTH),
            pl.BlockSpec((LRU_WIDTH // LRU_SUPER, LRU_SUPER, 2 * LRU_SUPER), lambda b, i: (0, 0, 0)),
            row(LRU_WIDTH), row(LRU_WIDTH), row(LRU_WIDTH),
        ],
        out_specs=pl.BlockSpec((1, tm, LRU_WIDTH), lambda b, i: (b, i, 0)),
        out_shape=jax.ShapeDtypeStruct((B, S, LRU_WIDTH), BF16),
        scratch_shapes=[pltpu.VMEM((SUBLANES, LRU_WIDTH), F32), pltpu.VMEM((SUBLANES, LRU_WIDTH), F32)],
        compiler_params=_params(("arbitrary", "arbitrary")),
        name="lru_branch",
    )(p_lru, cw, cb.reshape(1, -1), wbd, ba.reshape(1, -1), bx.reshape(1, -1), lam.reshape(1, -1))


def _rw_prep_kernel(*refs, tm, has_mix):
    if has_mix:
        (p_ref, mu_ref, w0_ref, a0_ref, w2_ref, a2_ref, g2_ref, vf_ref, v0_ref, v2_ref,
         r_ref, lw_ref, k_ref, v_ref, a_ref, g_ref, prev_ref) = refs
    else:
        (p_ref, mu_ref, w0_ref, a0_ref, w2_ref, a2_ref, g2_ref,
         r_ref, lw_ref, k_ref, v_ref, a_ref, g_ref, prev_ref) = refs

    @pl.when(pl.program_id(1) == 0)
    def _():
        prev_ref[...] = jnp.zeros_like(prev_ref)

    x = p_ref[0, :, :RW_COLS_PAD]
    row8 = lax.broadcasted_iota(jnp.int32, (SUBLANES, RW_COLS_PAD), 0)
    xs = _shift_rows(x, prev_ref[...], 1, row8)
    prev_ref[...] = x[tm - SUBLANES:]
    p = x + (xs - x) * mu_ref[...]

    r_ref[0] = p[:, 0:RW_WIDTH]
    k_ref[0] = p[:, RW_WIDTH:2 * RW_WIDTH]
    v = p[:, 2 * RW_WIDTH:3 * RW_WIDTH]
    lora = p[:, RW_LORA_OFF:RW_GATE_OFF]
    gl = p[:, RW_GATE_OFF:RW_COLS_PAD]

    w = w0_ref[...] + jnp.dot(jnp.tanh(lora).astype(BF16), w2_ref[...], preferred_element_type=F32)
    w = -_softplus(-w) - 0.5
    lw_ref[0] = -jnp.exp(w)
    a_ref[0] = _sigmoid(a0_ref[...] + jnp.dot(lora.astype(BF16), a2_ref[...], preferred_element_type=F32))
    g_ref[0] = jnp.dot(_sigmoid(gl).astype(BF16), g2_ref[...], preferred_element_type=F32)
    if has_mix:
        hv = p_ref[0, :, RW_COLS_PAD:RW_COLS_PAD + RW_MV_PAD]
        mix = _sigmoid(v0_ref[...] + jnp.dot(hv.astype(BF16), v2_ref[...], preferred_element_type=F32))
        v = v + (vf_ref[0] - v) * mix
    v_ref[0] = v


def _rw_prep(p_rw, mu, w0, a0, w2p, a2p, g2p, tm, mix=None):
    B, S, ncol = p_rw.shape
    has_mix = mix is not None
    const = lambda shape: pl.BlockSpec(shape, lambda b, i: (0,) * len(shape))
    tile = lambda n: pl.BlockSpec((1, tm, n), lambda b, i: (b, i, 0))
    in_specs = [tile(ncol), const((1, RW_COLS_PAD)), const((1, RW_WIDTH)), const((1, RW_WIDTH)),
                const((LANES, RW_WIDTH)), const((LANES, RW_WIDTH)), const((RW_GATE_PAD, RW_WIDTH))]
    args = [p_rw, mu, w0, a0, w2p, a2p, g2p]
    if has_mix:
        v_first, v0, v2p = mix
        in_specs += [tile(RW_WIDTH), const((1, RW_WIDTH)), const((RW_MV_PAD, RW_WIDTH))]
        args += [v_first, v0, v2p]
    out = jax.ShapeDtypeStruct((B, S, RW_WIDTH), F32)
    return pl.pallas_call(
        functools.partial(_rw_prep_kernel, tm=tm, has_mix=has_mix),
        grid=(B, S // tm),
        in_specs=in_specs,
        out_specs=[tile(RW_WIDTH)] * 6,
        out_shape=[out] * 6,
        scratch_shapes=[pltpu.VMEM((SUBLANES, RW_COLS_PAD), F32)],
        compiler_params=_params(("arbitrary", "arbitrary")),
        name="rw_prep",
    )(*args)


def _dot(a, b):
    return jnp.dot(a, b, preferred_element_type=F32)


def _dot_nt(a, b):
    return lax.dot_general(a, b, (((1,), (1,)), ((), ())), preferred_element_type=F32)


def _split(x):
    hi = x.astype(BF16)
    return hi, (x - hi.astype(F32)).astype(BF16)


def _seg_dot(x, seg):
    hi, lo = _split(x)
    return _dot(hi, seg) + _dot(lo, seg)


def _rw_scan_kernel(r_ref, lw_ref, k_ref, v_ref, a_ref, g_ref, kk_ref, ka_ref, rk_ref, lnw_ref, lnb_ref,
                    o_ref, m_ref, y_ref, *, ts):
    C = RW_CHUNK
    N = 2 * C

    @pl.when(pl.program_id(2) == 0)
    def _():
        m_ref[...] = jnp.zeros_like(m_ref)

    ri = lax.broadcasted_iota(jnp.int32, (N, N), 0)
    ci = lax.broadcasted_iota(jnp.int32, (N, N), 1)
    same = (ri >= C) == (ci >= C)
    strict = same & (ri > ci)
    incl = same & (ri >= ci)
    eye = (ri == ci).astype(F32)
    seg = same.astype(BF16)
    tri_bd = incl.astype(BF16)
    lane_lo = lax.broadcasted_iota(jnp.int32, (C, N), 1) < RW_HEAD

    def stack_masked(x):
        return jnp.concatenate([jnp.where(lane_lo, x, 0.0), jnp.where(lane_lo, 0.0, x)], axis=0)

    def stack(x):
        return jnp.concatenate([x, x], axis=0)

    kk_p = kk_ref[...]
    ka_p = ka_ref[...]

    chunks = range(ts // C)
    rows = lambda x, c: x[c * C:(c + 1) * C]
    r = r_ref[0]
    lw = lw_ref[0]
    v = v_ref[0]
    a = a_ref[0]
    kkr = k_ref[0] * kk_p
    kk = kkr * lax.rsqrt(jnp.maximum(_seg_dot(kkr * kkr, seg), 1e-24))
    k = k_ref[0] * (1.0 + (a - 1.0) * ka_p)
    b_s = kk * a

    lw_hi, lw_lo = _split(lw)
    wide = lambda x: jnp.concatenate([stack(rows(x, c)) for c in chunks], axis=1)
    cum_w = _dot(tri_bd, wide(lw_hi)) + _dot(tri_bd, wide(lw_lo))
    cum = jnp.concatenate([cum_w[:C, c * N:(c + 1) * N] for c in chunks], axis=0)
    last = [cum_w[C - 1:C, c * N:(c + 1) * N] for c in chunks]
    last_b = jnp.concatenate([jnp.broadcast_to(l, (C, N)) for l in last], axis=0)
    e_inv = jnp.exp(-cum)
    e_out = jnp.exp(last_b - cum)
    a_t = -kk * jnp.exp(cum - lw)
    r_t = r * jnp.exp(cum)
    b_in = b_s * e_inv
    k_in = k * e_inv
    b_out = b_s * e_out
    k_out = k * e_out

    a_st = [stack_masked(rows(a_t, c)).astype(BF16) for c in chunks]
    r_st = [stack_masked(rows(r_t, c)).astype(BF16) for c in chunks]
    v_st = [stack_masked(rows(v, c)).astype(BF16) for c in chunks]
    rhs = [jnp.concatenate([stack(rows(b_in, c)), stack(rows(k_in, c))], axis=0).astype(BF16) for c in chunks]
    gram = [_dot_nt(jnp.concatenate([a_st[c], r_st[c]], axis=0), rhs[c]) for c in chunks]
    ab = [jnp.where(strict, g[:N, :N], 0.0) for g in gram]
    ak = [jnp.where(strict, g[:N, N:], 0.0).astype(BF16) for g in gram]
    rb = [jnp.where(incl, g[N:, :N], 0.0).astype(BF16) for g in gram]
    rk = [jnp.where(incl, g[N:, N:], 0.0).astype(BF16) for g in gram]

    t_inv = [eye + x for x in ab]
    pw = ab
    for _ in range(5):
        pw_b = [x.astype(BF16) for x in pw]
        pw = [_dot(x, x) for x in pw_b]
        t_inv = [t + _dot(t.astype(BF16), x.astype(BF16)) for t, x in zip(t_inv, pw)]
    t_b = [t.astype(BF16) for t in t_inv]

    akv = [_dot(ak[c], v_st[c]).astype(BF16) for c in chunks]
    u0 = [_dot(t_b[c], akv[c]).astype(BF16) for c in chunks]
    w_st = [_dot(t_b[c], a_st[c]).astype(BF16) for c in chunks]
    bh_t = [stack_masked(rows(b_out, c)).T.astype(BF16) for c in chunks]
    kh_t = [stack_masked(rows(k_out, c)).T.astype(BF16) for c in chunks]
    m_lin = [_dot(bh_t[c], w_st[c]).astype(BF16) for c in chunks]
    y_lin = [(stack_masked(rows(r_t, c)) + _dot(rb[c], w_st[c])).astype(BF16) for c in chunks]
    m_add = [_dot(jnp.concatenate([bh_t[c], kh_t[c]], axis=1), jnp.concatenate([u0[c], v_st[c]], axis=0))
             for c in chunks]
    y_add = [_dot(jnp.concatenate([rb[c], rk[c]], axis=1), jnp.concatenate([u0[c], v_st[c]], axis=0))
             for c in chunks]
    lin = [jnp.concatenate([m_lin[c], y_lin[c]], axis=0) for c in chunks]
    decay_col = [jnp.broadcast_to(jnp.exp(l), (N, N)).T for l in last]

    m = m_ref[...]
    for c in chunks:
        x = _dot(lin[c], m.astype(BF16))
        y_st = x[N:] + y_add[c]
        m = decay_col[c] * m + x[:N] + m_add[c]
        y_ref[c * C:(c + 1) * C, :] = y_st[:C] + y_st[C:]
    m_ref[...] = m

    y = y_ref[...]
    mean = _seg_dot(y, seg) * (1.0 / RW_HEAD)
    yc = y - mean
    var = _seg_dot(yc * yc, seg) * (1.0 / RW_HEAD)
    yn = yc * lax.rsqrt(var + RW_LN_EPS) * lnw_ref[...] + lnb_ref[...]
    bonus = _seg_dot(r * k * rk_ref[...], seg) * v
    o_ref[0] = ((yn + bonus) * g_ref[0]).astype(o_ref.dtype)


def _rw_scan(r, lw, k, v, a, g, kk_p, ka_p, rk_p, lnw, lnb, ts):
    B, S, W = r.shape
    tile = pl.BlockSpec((1, ts, LANES), lambda b, h, i: (b, i, h))
    vec = pl.BlockSpec((1, LANES), lambda b, h, i: (0, h))
    return pl.pallas_call(
        functools.partial(_rw_scan_kernel, ts=ts),
        grid=(B, W // LANES, S // ts),
        in_specs=[tile] * 6 + [vec] * 5,
        out_specs=tile,
        out_shape=jax.ShapeDtypeStruct((B, S, W), BF16),
        scratch_shapes=[pltpu.VMEM((LANES, LANES), F32), pltpu.VMEM((ts, LANES), F32)],
        compiler_params=_params(("arbitrary", "arbitrary", "arbitrary")),
        name="rw_scan",
    )(r, lw, k, v, a, g, kk_p, ka_p, rk_p, lnw, lnb)


def _attn_kernel(q_ref, k_ref, v_ref, qg_ref, kg_ref, sink_ref, o_ref, kpad_ref, vpad_ref):
    W = WINDOW
    n = pl.program_id(1)

    @pl.when(n == 0)
    def _():
        kpad_ref[...] = jnp.zeros_like(kpad_ref)
        vpad_ref[...] = jnp.zeros_like(vpad_ref)

    kpad_ref[:, 0:W, :] = kpad_ref[:, W:2 * W, :]
    vpad_ref[:, 0:W, :] = vpad_ref[:, W:2 * W, :]

    ri = lax.broadcasted_iota(jnp.int32, (LANES, LANES), 0)
    ci = lax.broadcasted_iota(jnp.int32, (LANES, LANES), 1)
    seg = ((ri >= ATT_HEAD) == (ci >= ATT_HEAD)).astype(BF16)
    lo = lax.broadcasted_iota(jnp.int32, (W, LANES), 1) < ATT_HEAD
    band_lo = lax.broadcasted_iota(jnp.int32, (2 * W, LANES), 1) < ATT_HEAD
    ones_even = jnp.where(band_lo, 1.0, 0.0).astype(BF16)
    ones_odd = jnp.where(band_lo, 0.0, 1.0).astype(BF16)

    def head_rms(t, gain):
        ms = _seg_dot(t * t, seg) * (1.0 / ATT_HEAD)
        return t * lax.rsqrt(ms + QK_EPS) * gain

    kg = kg_ref[...]
    for c in range(KV_WIDTH // LANES):
        cs = slice(c * LANES, (c + 1) * LANES)
        kn = head_rms(k_ref[0, :, cs], kg)
        vv = v_ref[0, :, cs]
        for t, src in ((kpad_ref, kn), (vpad_ref, vv)):
            left = jnp.where(lo, src, 0.0)
            right = jnp.where(lo, 0.0, src)
            t[4 * c + 0, W:2 * W, :] = left.astype(BF16)
            t[4 * c + 1, W:2 * W, :] = pltpu.roll(left, ATT_HEAD, axis=1).astype(BF16)
            t[4 * c + 2, W:2 * W, :] = pltpu.roll(right, ATT_HEAD, axis=1).astype(BF16)
            t[4 * c + 3, W:2 * W, :] = right.astype(BF16)

    qi = lax.broadcasted_iota(jnp.int32, (W, 2 * W), 0) + W
    kj = lax.broadcasted_iota(jnp.int32, (W, 2 * W), 1)
    mask = (kj <= qi) & (qi - kj < W) & ((n * W - W + kj) >= 0)
    mask_add = jnp.where(mask, 0.0, NEG_INF)

    heads = range(ATT_HEADS)
    slot = [2 * (hd // ATT_GROUP) + hd % 2 for hd in heads]
    qg = qg_ref[...] * (ATT_HEAD ** -0.5)
    q = [head_rms(q_ref[0, :, p * LANES:(p + 1) * LANES], qg).astype(BF16) for p in range(ATT_HEADS // 2)]
    s = [_dot_nt(q[hd // 2], kpad_ref[slot[hd]]) + mask_add for hd in heads]
    sink = [sink_ref[hd:hd + 1, 0:1] for hd in heads]
    m = [jnp.broadcast_to(jnp.maximum(jnp.max(s[hd], axis=-1, keepdims=True), sink[hd]), (W, LANES))
         for hd in heads]
    e = [jnp.exp(s[hd] - jnp.concatenate([m[hd], m[hd]], axis=-1)).astype(BF16) for hd in heads]
    pv = [_dot(e[hd], vpad_ref[slot[hd]]) for hd in heads]
    rs = [_dot(e[hd], ones_odd if hd % 2 else ones_even) for hd in heads]
    sink_add = [jnp.exp(sink[hd] - m[hd]) for hd in heads]
    for p in range(ATT_HEADS // 2):
        num = pv[2 * p] + pv[2 * p + 1]
        den = rs[2 * p] + rs[2 * p + 1] + jnp.where(lo, sink_add[2 * p], sink_add[2 * p + 1])
        o_ref[0, :, p * LANES:(p + 1) * LANES] = (num / den).astype(o_ref.dtype)


def _attention(p_att, q_gain, k_gain, sinks_b):
    B, S, _ = p_att.shape
    nq = ATT_WIDTH // KV_WIDTH
    pair = lambda g: jnp.concatenate([g, g]).reshape(1, LANES)
    return pl.pallas_call(
        _attn_kernel,
        grid=(B, S // WINDOW),
        in_specs=[
            pl.BlockSpec((1, WINDOW, ATT_WIDTH), lambda b, n: (b, n, 0)),
            pl.BlockSpec((1, WINDOW, KV_WIDTH), lambda b, n: (b, n, nq)),
            pl.BlockSpec((1, WINDOW, KV_WIDTH), lambda b, n: (b, n, nq + 1)),
            pl.BlockSpec((1, LANES), lambda b, n: (0, 0)),
            pl.BlockSpec((1, LANES), lambda b, n: (0, 0)),
            pl.BlockSpec((ATT_HEADS, LANES), lambda b, n: (0, 0)),
        ],
        out_specs=pl.BlockSpec((1, WINDOW, ATT_WIDTH), lambda b, n: (b, n, 0)),
        out_shape=jax.ShapeDtypeStruct((B, S, ATT_WIDTH), BF16),
        scratch_shapes=[pltpu.VMEM((2 * ATT_KV_HEADS, 2 * WINDOW, LANES), BF16),
                        pltpu.VMEM((2 * ATT_KV_HEADS, 2 * WINDOW, LANES), BF16)],
        compiler_params=_params(("arbitrary", "arbitrary")),
        name="swa_attention",
    )(p_att, p_att, p_att, pair(q_gain), pair(k_gain), sinks_b)


def _merge_kernel(ya_ref, yb_ref, yc_ref, gt_ref, x_ref, g1_ref, wa_ref, wb_ref, wc_ref, wo_ref, o_ref):
    gt = gt_ref[0]
    o_a = jnp.dot(ya_ref[0], wa_ref[...], preferred_element_type=F32)
    mixed = _sigmoid(gt[:, 0:D_MODEL]) * o_a
    o_b = jnp.dot(yb_ref[0], wb_ref[...], preferred_element_type=F32)
    mixed = mixed + _sigmoid(gt[:, D_MODEL:2 * D_MODEL]) * o_b
    o_c = jnp.dot(yc_ref[0], wc_ref[...], preferred_element_type=F32)
    mixed = mixed + _sigmoid(gt[:, 2 * D_MODEL:3 * D_MODEL]) * o_c
    out = jnp.dot(mixed.astype(BF16), wo_ref[...], preferred_element_type=F32)
    o_ref[0] = x_ref[0] + g1_ref[0] * out


def _merge(ya, yb, yc, gates, x, g1, wa, wb, wc, wo, tm):
    B, S, D = x.shape
    tile = lambda n: pl.BlockSpec((1, tm, n), lambda b, i: (b, i, 0))
    const = lambda a: pl.BlockSpec(a.shape, lambda b, i: (0, 0))
    return pl.pallas_call(
        _merge_kernel,
        grid=(B, S // tm),
        in_specs=[tile(LRU_WIDTH), tile(RW_WIDTH), tile(ATT_WIDTH), tile(3 * D), tile(D),
                  pl.BlockSpec((1, 1, D), lambda b, i: (b, 0, 0)),
                  const(wa), const(wb), const(wc), const(wo)],
        out_specs=tile(D),
        out_shape=jax.ShapeDtypeStruct((B, S, D), F32),
        compiler_params=_params(("arbitrary", "arbitrary")),
        name="merge",
    )(ya, yb, yc, gates, x, g1, wa, wb, wc, wo)


FFN_CHUNKS = ((0, 1024), (1024, 1024), (2048, 768))


def _ffn_kernel(x_ref, sc_ref, sh_ref, g2_ref, wi_ref, wo_ref, o_ref):
    x = x_ref[0]
    ms = jnp.mean(x * x, axis=-1, keepdims=True)
    h = (x * lax.rsqrt(ms + RMS_EPS) * (1.0 + sc_ref[0]) + sh_ref[0]).astype(BF16)
    acc = jnp.zeros(x.shape, F32)
    for off, width in FFN_CHUNKS:
        gate = jnp.dot(h, wi_ref[:, off:off + width], preferred_element_type=F32)
        up = jnp.dot(h, wi_ref[:, D_FF + off:D_FF + off + width], preferred_element_type=F32)
        act = (gate * _sigmoid(gate) * up).astype(BF16)
        acc = acc + jnp.dot(act, wo_ref[off:off + width, :], preferred_element_type=F32)
    o_ref[0] = x + g2_ref[0] * acc


def _ffn(x, sc, sh, g2, wi, wo, tm):
    B, S, D = x.shape
    tile = pl.BlockSpec((1, tm, D), lambda b, i: (b, i, 0))
    vec = pl.BlockSpec((1, 1, D), lambda b, i: (b, 0, 0))
    return pl.pallas_call(
        _ffn_kernel,
        grid=(B, S // tm),
        in_specs=[tile, vec, vec, vec,
                  pl.BlockSpec(wi.shape, lambda b, i: (0, 0)),
                  pl.BlockSpec(wo.shape, lambda b, i: (0, 0))],
        out_specs=tile,
        out_shape=jax.ShapeDtypeStruct((B, S, D), F32),
        compiler_params=_params(("arbitrary", "arbitrary")),
        name="ffn",
    )(x, sc, sh, g2, wi, wo)


def _pad_rows(w, rows):
    return jnp.pad(w, ((0, rows - w.shape[0]), (0, 0)))


def _pad_cols(w, cols):
    return jnp.pad(w, ((0, 0), (0, cols - w.shape[1])))


def _block_diag_gates(wa, wx):
    per = LRU_SUPER // LRU_BLOCK
    out = []
    for s in range(LRU_WIDTH // LRU_SUPER):
        halves = []
        for w in (wa, wx):
            m = jnp.zeros((LRU_SUPER, LRU_SUPER), w.dtype)
            for n in range(per):
                m = lax.dynamic_update_slice(m, w[s * per + n], (n * LRU_BLOCK, n * LRU_BLOCK))
            halves.append(m)
        out.append(jnp.concatenate(halves, axis=1))
    return jnp.stack(out).astype(BF16)


def _tile_rows(S, want):
    t = min(want, S)
    while S % t:
        t //= 2
    return t


def kernel(x, c, w_ada, b_ada, w_in, conv_w, conv_b, lru_wa, lru_ba, lru_wx, lru_bx, lru_lambda, w_lru_o, rw_mu, rw_w0, rw_w2, rw_a0, rw_a2, rw_g2, rw_kk, rw_ka, rw_rk, rw_ln_w, rw_ln_b, rw_v0, rw_v1, rw_v2, w_rw_o, q_gain, k_gain, sinks, w_att_o, w_out, w_ffn_in, w_ffn_out):
    B, S, D = x.shape
    depth = w_in.shape[0]
    assert D == D_MODEL and S % WINDOW == 0
    tm_mm = _tile_rows(S, 512)
    tm_seq = _tile_rows(S, 256)
    ts_scan = _tile_rows(S, 512)

    mod = _ada(c, w_ada, b_ada).reshape(depth, B, 6, 1, D)

    o_lru = 2 * LRU_WIDTH
    o_rw = o_lru + 3 * RW_WIDTH + RW_DECAY_LORA + RW_AAA_LORA + RW_GATE_LORA
    o_att = o_rw + ATT_WIDTH + 2 * KV_WIDTH

    v_first = None
    for i in range(depth):
        sh1, sc1, g1, sh2, sc2, g2 = (mod[i, :, j] for j in range(6))
        wi = w_in[i]
        w_lru = wi[:, :o_lru].astype(BF16)
        w_rw_cols = [_pad_cols(wi[:, o_lru:o_rw], RW_COLS_PAD)]
        if i > 0:
            w_rw_cols.append(_pad_cols(rw_v1[i - 1], RW_MV_PAD))
        w_rw = jnp.concatenate(w_rw_cols, axis=1).astype(BF16)
        w_att = wi[:, o_rw:o_att].astype(BF16)
        w_gate = wi[:, o_att:].astype(BF16)

        h = _norm_mod(x, sc1, sh1, tm_mm)
        p_lru = _mm(h, w_lru, tm_mm, F32, "proj_lru")
        p_rw = _mm(h, w_rw, tm_mm, F32, "proj_rw")
        p_att = _mm(h, w_att, tm_mm, F32, "proj_att")
        p_gate = _mm(h, w_gate, tm_mm, F32, "proj_gate")

        ya = _lru_branch(p_lru, conv_w[i], conv_b[i], _block_diag_gates(lru_wa[i], lru_wx[i]),
                         lru_ba[i], lru_bx[i], lru_lambda[i], tm_seq)

        mu = _pad_cols(rw_mu[i].reshape(1, -1), RW_COLS_PAD)
        w2p = _pad_rows(rw_w2[i], LANES).astype(BF16)
        a2p = jnp.concatenate([jnp.zeros_like(rw_a2[i]), rw_a2[i]], axis=0).astype(BF16)
        g2p = _pad_rows(rw_g2[i], RW_GATE_PAD).astype(BF16)
        mix = None
        if i > 0:
            mix = (v_first, rw_v0[i - 1].reshape(1, -1), _pad_rows(rw_v2[i - 1], RW_MV_PAD).astype(BF16))
        r, lw, k, v, a, g = _rw_prep(p_rw, mu, rw_w0[i].reshape(1, -1), rw_a0[i].reshape(1, -1),
                                     w2p, a2p, g2p, tm_seq, mix)
        if i == 0:
            v_first = v
        yb = _rw_scan(r, lw, k, v, a, g, rw_kk[i].reshape(1, -1), rw_ka[i].reshape(1, -1),
                      rw_rk[i].reshape(1, -1), rw_ln_w[i].reshape(1, -1), rw_ln_b[i].reshape(1, -1), ts_scan)

        sinks_b = jnp.broadcast_to(sinks[i].reshape(-1, 1), (ATT_HEADS, LANES))
        yc = _attention(p_att, q_gain[i], k_gain[i], sinks_b)

        x = _merge(ya, yb, yc, p_gate, x, g1, w_lru_o[i].astype(BF16), w_rw_o[i].astype(BF16),
                   w_att_o[i].astype(BF16), w_out[i].astype(BF16), tm_mm)
        x = _ffn(x, sc2, sh2, g2, w_ffn_in[i].astype(BF16), w_ffn_out[i].astype(BF16), tm_mm)
    return x
```

```python
import functools

import jax
import jax.numpy as jnp
from jax import lax
from jax.experimental import pallas as pl
from jax.experimental.pallas import tpu as pltpu

D_MODEL = 1024
LRU_WIDTH = 1280
LRU_BLOCKS = 16
LRU_BLOCK = 80
LRU_SUPER = 640
CONV_WIDTH = 4
LRU_C = 8.0
RW_HEAD = 64
RW_WIDTH = 1024
RW_DECAY_LORA = 64
RW_AAA_LORA = 64
RW_MV_LORA = 32
RW_GATE_LORA = 160
RW_LN_EPS = 64e-5
ATT_HEAD = 64
ATT_HEADS = 16
ATT_KV_HEADS = 4
ATT_GROUP = 4
ATT_WIDTH = 1024
KV_WIDTH = 256
WINDOW = 128
D_FF = 2816
RMS_EPS = 1e-6
QK_EPS = 1e-6
NEG_INF = -1e30

LANES = 128
SUBLANES = 8
VMEM_LIMIT_BYTES = 56 * 1024 * 1024

RW_LORA_OFF = 3 * RW_WIDTH
RW_GATE_OFF = RW_LORA_OFF + LANES
RW_GATE_PAD = 2 * LANES
RW_COLS_PAD = RW_GATE_OFF + RW_GATE_PAD
RW_MV_PAD = LANES

RW_CHUNK = 64
RW_GROUP = 8

F32 = jnp.float32
BF16 = jnp.bfloat16
HIGHEST = lax.Precision.HIGHEST


def _sigmoid(x):
    return 0.5 * jnp.tanh(0.5 * x) + 0.5


def _softplus(z):
    return jnp.maximum(z, 0.0) + jnp.log(1.0 + jnp.exp(-jnp.abs(z)))


def _params(sem):
    return pltpu.CompilerParams(dimension_semantics=sem, vmem_limit_bytes=VMEM_LIMIT_BYTES)


def _ada_kernel(c_ref, w_ref, b_ref, o_ref):
    c = c_ref[...]
    act = (c * _sigmoid(c)).astype(BF16)
    o_ref[0] = jnp.dot(act, w_ref[0].astype(BF16), preferred_element_type=F32) + b_ref[0]


def _ada(c, w_ada, b_ada):
    L, D, N = w_ada.shape
    B = c.shape[0]
    tn = D_MODEL
    return pl.pallas_call(
        _ada_kernel,
        grid=(L, N // tn),
        in_specs=[
            pl.BlockSpec((B, D), lambda l, j: (0, 0)),
            pl.BlockSpec((1, D, tn), lambda l, j: (l, 0, j)),
            pl.BlockSpec((1, 1, tn), lambda l, j: (l, 0, j)),
        ],
        out_specs=pl.BlockSpec((1, B, tn), lambda l, j: (l, 0, j)),
        out_shape=jax.ShapeDtypeStruct((L, B, N), F32),
        compiler_params=_params(("arbitrary", "arbitrary")),
        name="ada_mod",
    )(c, w_ada, b_ada.reshape(L, 1, N))


def _norm_mod_kernel(x_ref, sc_ref, sh_ref, o_ref):
    x = x_ref[0]
    ms = jnp.mean(x * x, axis=-1, keepdims=True)
    h = x * lax.rsqrt(ms + RMS_EPS) * (1.0 + sc_ref[0]) + sh_ref[0]
    o_ref[0] = h.astype(o_ref.dtype)


def _norm_mod(x, sc, sh, tm):
    B, S, D = x.shape
    vec = pl.BlockSpec((1, 1, D), lambda b, i: (b, 0, 0))
    return pl.pallas_call(
        _norm_mod_kernel,
        grid=(B, S // tm),
        in_specs=[pl.BlockSpec((1, tm, D), lambda b, i: (b, i, 0)), vec, vec],
        out_specs=pl.BlockSpec((1, tm, D), lambda b, i: (b, i, 0)),
        out_shape=jax.ShapeDtypeStruct((B, S, D), BF16),
        compiler_params=_params(("arbitrary", "arbitrary")),
        name="norm_mod",
    )(x, sc, sh)


def _mm_kernel(x_ref, w_ref, o_ref, *, n_chunk):
    x = x_ref[0]
    n = o_ref.shape[-1]
    for j in range(0, n, n_chunk):
        w = min(n_chunk, n - j)
        o_ref[0, :, j:j + w] = jnp.dot(x, w_ref[:, j:j + w], preferred_element_type=F32).astype(o_ref.dtype)


def _mm(x, w, tm, out_dtype, name):
    B, S, K = x.shape
    N = w.shape[1]
    return pl.pallas_call(
        functools.partial(_mm_kernel, n_chunk=512),
        grid=(B, S // tm),
        in_specs=[
            pl.BlockSpec((1, tm, K), lambda b, i: (b, i, 0)),
            pl.BlockSpec((K, N), lambda b, i: (0, 0)),
        ],
        out_specs=pl.BlockSpec((1, tm, N), lambda b, i: (b, i, 0)),
        out_shape=jax.ShapeDtypeStruct((B, S, N), out_dtype),
        compiler_params=_params(("arbitrary", "arbitrary")),
        name=name,
    )(x, w)


def _shift_rows(x, prev8, j, row8):
    xs = pltpu.roll(x, j, axis=0)
    ps = pltpu.roll(prev8, j, axis=0)
    head = jnp.where(row8 < j, ps, xs[:SUBLANES])
    return jnp.concatenate([head, xs[SUBLANES:]], axis=0)


def _lru_kernel(p_ref, cw_ref, cb_ref, wbd_ref, ba_ref, bx_ref, lam_ref, o_ref, xbuf_ref, h_ref, *, tm):
    @pl.when(pl.program_id(1) == 0)
    def _():
        xbuf_ref[...] = jnp.zeros_like(xbuf_ref)
        h_ref[...] = jnp.zeros_like(h_ref)

    x = p_ref[0, :, :LRU_WIDTH]
    y = p_ref[0, :, LRU_WIDTH:]
    groups = tm // SUBLANES
    xg = jnp.concatenate([xbuf_ref[...], x], axis=0).reshape(groups + 1, SUBLANES, LRU_WIDTH)
    xbuf_ref[...] = x[tm - SUBLANES:]
    sub = lax.broadcasted_iota(jnp.int32, (groups, SUBLANES, LRU_WIDTH), 1)
    cw = cw_ref[...]
    xa = cb_ref[...] + x * cw[CONV_WIDTH - 1:CONV_WIDTH]
    for j in range(1, CONV_WIDTH):
        rot = pltpu.roll(xg, j, axis=1)
        delayed = jnp.where(sub >= j, rot[1:], rot[:-1]).reshape(tm, LRU_WIDTH)
        xa = xa + delayed * cw[CONV_WIDTH - 1 - j:CONV_WIDTH - j]

    xb = xa.astype(BF16)
    pre = [jnp.dot(xb[:, s * LRU_SUPER:(s + 1) * LRU_SUPER], wbd_ref[s], preferred_element_type=F32)
           for s in range(LRU_WIDTH // LRU_SUPER)]
    pre_a = jnp.concatenate([p[:, :LRU_SUPER] for p in pre], axis=-1)
    pre_x = jnp.concatenate([p[:, LRU_SUPER:] for p in pre], axis=-1)
    gate_a = _sigmoid(pre_a + ba_ref[...])
    gate_x = _sigmoid(pre_x + bx_ref[...])
    log_a = -LRU_C * gate_a * _softplus(-lam_ref[...])
    a = jnp.exp(log_a)
    z = 1.0 - a * a
    u = xa * gate_x * (z * lax.rsqrt(jnp.maximum(z, 1e-30)))

    a = a.reshape(groups, SUBLANES, LRU_WIDTH)
    u = u.reshape(groups, SUBLANES, LRU_WIDTH)
    d = 1
    while d < SUBLANES:
        keep = sub >= d
        u = jnp.where(keep, a * pltpu.roll(u, d, axis=1) + u, u)
        a = jnp.where(keep, a * pltpu.roll(a, d, axis=1), a)
        d *= 2
    carry = h_ref[0:1, :]
    hs = []
    for g in range(groups):
        hs.append(a[g] * carry + u[g])
        carry = hs[-1][SUBLANES - 1:SUBLANES, :]
    h = jnp.concatenate(hs, axis=0)
    h_ref[...] = jnp.broadcast_to(carry, h_ref.shape)

    gelu = 0.5 * y * (1.0 + jnp.tanh(0.7978845608028654 * (y + 0.044715 * (y * y * y))))
    o_ref[0] = (h * gelu).astype(o_ref.dtype)


def _lru_branch(p_lru, cw, cb, wbd, ba, bx, lam, tm):
    B, S, _ = p_lru.shape
    row = lambda n: pl.BlockSpec((1, n), lambda b, i: (0, 0))
    return pl.pallas_call(
        functools.partial(_lru_kernel, tm=tm),
        grid=(B, S // tm),
        in_specs=[
            pl.BlockSpec((1, tm, 2 * LRU_WIDTH), lambda b, i: (b, i, 0)),
            pl.BlockSpec((CONV_WIDTH, LRU_WIDTH), lambda b, i: (0, 0)),
            row(LRU_WIDTH),
            pl.BlockSpec((LRU_WIDTH // LRU_SUPER, LRU_SUPER, 2 * LRU_SUPER), lambda b, i: (0, 0, 0)),
            row(LRU_WIDTH), row(LRU_WIDTH), row(LRU_WIDTH),
        ],
        out_specs=pl.BlockSpec((1, tm, LRU_WIDTH), lambda b, i: (b, i, 0)),
        out_shape=jax.ShapeDtypeStruct((B, S, LRU_WIDTH), BF16),
        scratch_shapes=[pltpu.VMEM((SUBLANES, LRU_WIDTH), F32), pltpu.VMEM((SUBLANES, LRU_WIDTH), F32)],
        compiler_params=_params(("arbitrary", "arbitrary")),
        name="lru_branch",
    )(p_lru, cw, cb.reshape(1, -1), wbd, ba.reshape(1, -1), bx.reshape(1, -1), lam.reshape(1, -1))


def _rw_prep_kernel(*refs, tm, has_mix):
    if has_mix:
        (p_ref, mu_ref, w0_ref, a0_ref, w2_ref, a2_ref, g2_ref, vf_ref, v0_ref, v2_ref,
         r_ref, lw_ref, k_ref, v_ref, a_ref, g_ref, prev_ref) = refs
    else:
        (p_ref, mu_ref, w0_ref, a0_ref, w2_ref, a2_ref, g2_ref,
         r_ref, lw_ref, k_ref, v_ref, a_ref, g_ref, prev_ref) = refs

    @pl.when(pl.program_id(1) == 0)
    def _():
        prev_ref[...] = jnp.zeros_like(prev_ref)

    x = p_ref[0, :, :RW_COLS_PAD]
    row8 = lax.broadcasted_iota(jnp.int32, (SUBLANES, RW_COLS_PAD), 0)
    xs = _shift_rows(x, prev_ref[...], 1, row8)
    prev_ref[...] = x[tm - SUBLANES:]
    p = x + (xs - x) * mu_ref[...]

    r_ref[0] = p[:, 0:RW_WIDTH]
    k_ref[0] = p[:, RW_WIDTH:2 * RW_WIDTH]
    v = p[:, 2 * RW_WIDTH:3 * RW_WIDTH]
    lora = p[:, RW_LORA_OFF:RW_GATE_OFF]
    gl = p[:, RW_GATE_OFF:RW_COLS_PAD]

    w = w0_ref[...] + jnp.dot(jnp.tanh(lora).astype(BF16), w2_ref[...], preferred_element_type=F32)
    w = -_softplus(-w) - 0.5
    lw_ref[0] = -jnp.exp(w)
    a_ref[0] = _sigmoid(a0_ref[...] + jnp.dot(lora.astype(BF16), a2_ref[...], preferred_element_type=F32))
    g_ref[0] = jnp.dot(_sigmoid(gl).astype(BF16), g2_ref[...], preferred_element_type=F32)
    if has_mix:
        hv = p_ref[0, :, RW_COLS_PAD:RW_COLS_PAD + RW_MV_PAD]
        mix = _sigmoid(v0_ref[...] + jnp.dot(hv.astype(BF16), v2_ref[...], preferred_element_type=F32))
        v = v + (vf_ref[0] - v) * mix
    v_ref[0] = v


def _rw_prep(p_rw, mu, w0, a0, w2p, a2p, g2p, tm, mix=None):
    B, S, ncol = p_rw.shape
    has_mix = mix is not None
    const = lambda shape: pl.BlockSpec(shape, lambda b, i: (0,) * len(shape))
    tile = lambda n: pl.BlockSpec((1, tm, n), lambda b, i: (b, i, 0))
    in_specs = [tile(ncol), const((1, RW_COLS_PAD)), const((1, RW_WIDTH)), const((1, RW_WIDTH)),
                const((LANES, RW_WIDTH)), const((LANES, RW_WIDTH)), const((RW_GATE_PAD, RW_WIDTH))]
    args = [p_rw, mu, w0, a0, w2p, a2p, g2p]
    if has_mix:
        v_first, v0, v2p = mix
        in_specs += [tile(RW_WIDTH), const((1, RW_WIDTH)), const((RW_MV_PAD, RW_WIDTH))]
        args += [v_first, v0, v2p]
    out = jax.ShapeDtypeStruct((B, S, RW_WIDTH), F32)
    return pl.pallas_call(
        functools.partial(_rw_prep_kernel, tm=tm, has_mix=has_mix),
        grid=(B, S // tm),
        in_specs=in_specs,
        out_specs=[tile(RW_WIDTH)] * 6,
        out_shape=[out] * 6,
        scratch_shapes=[pltpu.VMEM((SUBLANES, RW_COLS_PAD), F32)],
        compiler_params=_params(("arbitrary", "arbitrary")),
        name="rw_prep",
    )(*args)


def _dot(a, b):
    return jnp.dot(a, b, preferred_element_type=F32)


def _dot_nt(a, b):
    return lax.dot_general(a, b, (((1,), (1,)), ((), ())), preferred_element_type=F32)


def _split(x):
    hi = x.astype(BF16)
    return hi, (x - hi.astype(F32)).astype(BF16)


def _seg_dot(x, seg2):
    hi, lo = _split(x)
    return _dot(jnp.concatenate([hi, lo], axis=1), seg2)


def _rw_scan_kernel(r_ref, lw_ref, k_ref, v_ref, a_ref, g_ref, kk_ref, ka_ref, rk_ref, lnw_ref, lnb_ref,
                    o_ref, m_ref, y_ref, *, ts):
    C = RW_CHUNK
    N = 2 * C

    @pl.when(pl.program_id(2) == 0)
    def _():
        m_ref[...] = jnp.zeros_like(m_ref)

    ri = lax.broadcasted_iota(jnp.int32, (N, N), 0)
    ci = lax.broadcasted_iota(jnp.int32, (N, N), 1)
    same = (ri >= C) == (ci >= C)
    strict = same & (ri > ci)
    incl = same & (ri >= ci)
    eye = (ri == ci).astype(F32)
    seg = jnp.concatenate([same.astype(BF16)] * 2, axis=0)
    tri2 = jnp.concatenate([incl.astype(BF16)] * 2, axis=1)
    lane_lo = lax.broadcasted_iota(jnp.int32, (C, N), 1) < RW_HEAD

    def stack_masked(x):
        return jnp.concatenate([jnp.where(lane_lo, x, 0.0), jnp.where(lane_lo, 0.0, x)], axis=0)

    def stack(x):
        return jnp.concatenate([x, x], axis=0)

    kk_p = kk_ref[...]
    ka_p = ka_ref[...]
    rows = lambda x, c: x[c * C:(c + 1) * C]

    def phase1(first):
        chunks = range(RW_GROUP)
        gs = slice(first * C, (first + RW_GROUP) * C)
        r = r_ref[0, gs, :]
        lw = lw_ref[0, gs, :]
        v = v_ref[0, gs, :]
        a = a_ref[0, gs, :]
        kkr = k_ref[0, gs, :] * kk_p
        kk = kkr * lax.rsqrt(jnp.maximum(_seg_dot(kkr * kkr, seg), 1e-24))
        k = k_ref[0, gs, :] * (1.0 + (a - 1.0) * ka_p)
        b_s = kk * a

        lw_hi, lw_lo = _split(lw)
        wide = lambda x: jnp.concatenate([stack(rows(x, c)) for c in chunks], axis=1)
        cum_w = _dot(tri2, jnp.concatenate([wide(lw_hi), wide(lw_lo)], axis=0))
        cum = jnp.concatenate([cum_w[:C, c * N:(c + 1) * N] for c in chunks], axis=0)
        last = [cum_w[C - 1:C, c * N:(c + 1) * N] for c in chunks]
        last_b = jnp.concatenate([jnp.broadcast_to(l, (C, N)) for l in last], axis=0)
        e_inv = jnp.exp(-cum)
        e_out = jnp.exp(last_b - cum)
        a_t = -kk * jnp.exp(cum - lw)
        r_t = r * jnp.exp(cum)
        b_in = b_s * e_inv
        k_in = k * e_inv
        b_out = b_s * e_out
        k_out = k * e_out
        yield None

        a_st = [stack_masked(rows(a_t, c)).astype(BF16) for c in chunks]
        r_st = [stack_masked(rows(r_t, c)).astype(BF16) for c in chunks]
        v_st = [stack_masked(rows(v, c)).astype(BF16) for c in chunks]
        rhs = [jnp.concatenate([stack(rows(b_in, c)), stack(rows(k_in, c))], axis=0).astype(BF16) for c in chunks]
        gram = [_dot_nt(jnp.concatenate([a_st[c], r_st[c]], axis=0), rhs[c]) for c in chunks]
        ab = [jnp.where(strict, g[:N, :N], 0.0) for g in gram]
        ak = [jnp.where(strict, g[:N, N:], 0.0).astype(BF16) for g in gram]
        rb = [jnp.where(incl, g[N:, :N], 0.0).astype(BF16) for g in gram]
        rk = [jnp.where(incl, g[N:, N:], 0.0).astype(BF16) for g in gram]
        yield None

        s_inv = [eye + x for x in ab]
        q = [x.astype(BF16) for x in ab]
        q = [_dot(x, x) for x in q]
        yield None
        for _ in range(4):
            q_b = [x.astype(BF16) for x in q]
            prod = [_dot(q_b[c], jnp.concatenate([q_b[c], s_inv[c].astype(BF16)], axis=1)) for c in chunks]
            q = [p[:, :N] for p in prod]
            s_inv = [s + p[:, N:] for s, p in zip(s_inv, prod)]
            yield None
        t_b = [(s + _dot(x.astype(BF16), s.astype(BF16))).astype(BF16) for s, x in zip(s_inv, q)]
        yield None

        akv = [_dot(ak[c], v_st[c]).astype(BF16) for c in chunks]
        yield None
        uw = [_dot(t_b[c], jnp.concatenate([akv[c], a_st[c]], axis=1)).astype(BF16) for c in chunks]
        bh_t = [stack_masked(rows(b_out, c)).T.astype(BF16) for c in chunks]
        kh_t = [stack_masked(rows(k_out, c)).T.astype(BF16) for c in chunks]
        yield None
        zero = jnp.zeros((N, N), BF16)
        left = [jnp.concatenate([jnp.concatenate([bh_t[c], kh_t[c]], axis=1),
                                 jnp.concatenate([rb[c], rk[c]], axis=1)], axis=0) for c in chunks]
        right = [jnp.concatenate([jnp.concatenate([uw[c][:, N:], uw[c][:, :N]], axis=1),
                                  jnp.concatenate([zero, v_st[c]], axis=1)], axis=0) for c in chunks]
        both = [_dot(left[c], right[c]) for c in chunks]
        lin = [jnp.concatenate([both[c][:N, :N], stack_masked(rows(r_t, c)) + both[c][N:, :N]],
                               axis=0).astype(BF16) for c in chunks]
        decay_col = [jnp.broadcast_to(jnp.exp(l), (N, N)).T for l in last]
        yield [(first + c, lin[c], both[c][:N, N:], both[c][N:, N:], decay_col[c]) for c in chunks]

    state = [m_ref[...]]

    def phase2_step(item):
        c, lin, m_add, y_add, decay_col = item
        m = state[0]
        x = _dot(lin, m.astype(BF16))
        y_st = x[N:] + y_add
        state[0] = decay_col * m + x[:N] + m_add
        y_ref[c * C:(c + 1) * C, :] = y_st[:C] + y_st[C:]

    pending = []
    for first in range(0, ts // C, RW_GROUP):
        maps = None
        for item in phase1(first):
            if item is not None:
                maps = item
            elif pending:
                phase2_step(pending.pop(0))
        while pending:
            phase2_step(pending.pop(0))
        pending = list(maps)
    while pending:
        phase2_step(pending.pop(0))
    m_ref[...] = state[0]

    y = y_ref[...]
    a = a_ref[0]
    k = k_ref[0] * (1.0 + (a - 1.0) * ka_p)
    mean = _seg_dot(y, seg) * (1.0 / RW_HEAD)
    yc = y - mean
    var = _seg_dot(yc * yc, seg) * (1.0 / RW_HEAD)
    yn = yc * lax.rsqrt(var + RW_LN_EPS) * lnw_ref[...] + lnb_ref[...]
    bonus = _seg_dot(r_ref[0] * k * rk_ref[...], seg) * v_ref[0]
    o_ref[0] = ((yn + bonus) * g_ref[0]).astype(o_ref.dtype)


def _rw_scan(r, lw, k, v, a, g, kk_p, ka_p, rk_p, lnw, lnb, ts):
    B, S, W = r.shape
    tile = pl.BlockSpec((1, ts, LANES), lambda b, h, i: (b, i, h))
    vec = pl.BlockSpec((1, LANES), lambda b, h, i: (0, h))
    return pl.pallas_call(
        functools.partial(_rw_scan_kernel, ts=ts),
        grid=(B, W // LANES, S // ts),
        in_specs=[tile] * 6 + [vec] * 5,
        out_specs=tile,
        out_shape=jax.ShapeDtypeStruct((B, S, W), BF16),
        scratch_shapes=[pltpu.VMEM((LANES, LANES), F32), pltpu.VMEM((ts, LANES), F32)],
        compiler_params=_params(("arbitrary", "arbitrary", "arbitrary")),
        name="rw_scan",
    )(r, lw, k, v, a, g, kk_p, ka_p, rk_p, lnw, lnb)


def _attn_kernel(q_ref, k_ref, v_ref, qg_ref, kg_ref, sink_ref, o_ref, kpad_ref, vpad_ref):
    W = WINDOW
    n = pl.program_id(1)

    band_lo = lax.broadcasted_iota(jnp.int32, (2 * W, LANES), 1) < ATT_HEAD
    ones_even = jnp.where(band_lo, 1.0, 0.0).astype(BF16)
    ones_odd = jnp.where(band_lo, 0.0, 1.0).astype(BF16)

    @pl.when(n == 0)
    def _():
        kpad_ref[...] = jnp.zeros_like(kpad_ref)
        for t in range(2 * ATT_KV_HEADS):
            vpad_ref[t, :, 0:LANES] = jnp.zeros((2 * W, LANES), BF16)
            vpad_ref[t, :, LANES:2 * LANES] = ones_odd if t % 2 else ones_even

    kpad_ref[:, 0:W, :] = kpad_ref[:, W:2 * W, :]
    vpad_ref[:, 0:W, 0:LANES] = vpad_ref[:, W:2 * W, 0:LANES]

    ri = lax.broadcasted_iota(jnp.int32, (LANES, LANES), 0)
    ci = lax.broadcasted_iota(jnp.int32, (LANES, LANES), 1)
    seg = jnp.concatenate([((ri >= ATT_HEAD) == (ci >= ATT_HEAD)).astype(BF16)] * 2, axis=0)
    lo = lax.broadcasted_iota(jnp.int32, (W, LANES), 1) < ATT_HEAD

    def head_rms(t, gain):
        ms = _seg_dot(t * t, seg) * (1.0 / ATT_HEAD)
        return t * lax.rsqrt(ms + QK_EPS) * gain

    kg = kg_ref[...]
    for c in range(KV_WIDTH // LANES):
        cs = slice(c * LANES, (c + 1) * LANES)
        kn = head_rms(k_ref[0, :, cs], kg)
        vv = v_ref[0, :, cs]
        for t, src in ((kpad_ref, kn), (vpad_ref, vv)):
            left = jnp.where(lo, src, 0.0)
            right = jnp.where(lo, 0.0, src)
            t[4 * c + 0, W:2 * W, 0:LANES] = left.astype(BF16)
            t[4 * c + 1, W:2 * W, 0:LANES] = pltpu.roll(left, ATT_HEAD, axis=1).astype(BF16)
            t[4 * c + 2, W:2 * W, 0:LANES] = pltpu.roll(right, ATT_HEAD, axis=1).astype(BF16)
            t[4 * c + 3, W:2 * W, 0:LANES] = right.astype(BF16)

    qi = lax.broadcasted_iota(jnp.int32, (W, 2 * W), 0) + W
    kj = lax.broadcasted_iota(jnp.int32, (W, 2 * W), 1)
    mask = (kj <= qi) & (qi - kj < W) & ((n * W - W + kj) >= 0)
    mask_add = jnp.where(mask, 0.0, NEG_INF)

    heads = range(ATT_HEADS)
    slot = [2 * (hd // ATT_GROUP) + hd % 2 for hd in heads]
    qg = qg_ref[...] * (ATT_HEAD ** -0.5)
    q = [head_rms(q_ref[0, :, p * LANES:(p + 1) * LANES], qg).astype(BF16) for p in range(ATT_HEADS // 2)]
    s = [_dot_nt(q[hd // 2], kpad_ref[slot[hd]]) + mask_add for hd in heads]
    sink = [sink_ref[hd:hd + 1, 0:1] for hd in heads]
    m = [jnp.broadcast_to(jnp.maximum(jnp.max(s[hd], axis=-1, keepdims=True), sink[hd]), (W, LANES))
         for hd in heads]
    e = [jnp.exp(s[hd] - jnp.concatenate([m[hd], m[hd]], axis=-1)).astype(BF16) for hd in heads]
    pv = [_dot(e[hd], vpad_ref[slot[hd]]) for hd in heads]
    sink_add = [jnp.exp(sink[hd] - m[hd]) for hd in heads]
    for p in range(ATT_HEADS // 2):
        both = pv[2 * p] + pv[2 * p + 1]
        den = both[:, LANES:] + jnp.where(lo, sink_add[2 * p], sink_add[2 * p + 1])
        o_ref[0, :, p * LANES:(p + 1) * LANES] = (both[:, :LANES] / den).astype(o_ref.dtype)


def _attention(p_att, q_gain, k_gain, sinks_b):
    B, S, _ = p_att.shape
    nq = ATT_WIDTH // KV_WIDTH
    pair = lambda g: jnp.concatenate([g, g]).reshape(1, LANES)
    return pl.pallas_call(
        _attn_kernel,
        grid=(B, S // WINDOW),
        in_specs=[
            pl.BlockSpec((1, WINDOW, ATT_WIDTH), lambda b, n: (b, n, 0)),
            pl.BlockSpec((1, WINDOW, KV_WIDTH), lambda b, n: (b, n, nq)),
            pl.BlockSpec((1, WINDOW, KV_WIDTH), lambda b, n: (b, n, nq + 1)),
            pl.BlockSpec((1, LANES), lambda b, n: (0, 0)),
            pl.BlockSpec((1, LANES), lambda b, n: (0, 0)),
            pl.BlockSpec((ATT_HEADS, LANES), lambda b, n: (0, 0)),
        ],
        out_specs=pl.BlockSpec((1, WINDOW, ATT_WIDTH), lambda b, n: (b, n, 0)),
        out_shape=jax.ShapeDtypeStruct((B, S, ATT_WIDTH), BF16),
        scratch_shapes=[pltpu.VMEM((2 * ATT_KV_HEADS, 2 * WINDOW, LANES), BF16),
                        pltpu.VMEM((2 * ATT_KV_HEADS, 2 * WINDOW, 2 * LANES), BF16)],
        compiler_params=_params(("arbitrary", "arbitrary")),
        name="swa_attention",
    )(p_att, p_att, p_att, pair(q_gain), pair(k_gain), sinks_b)


def _merge_kernel(ya_ref, yb_ref, yc_ref, gt_ref, x_ref, g1_ref, wa_ref, wb_ref, wc_ref, wo_ref, o_ref):
    gt = gt_ref[0]
    o_a = jnp.dot(ya_ref[0], wa_ref[...], preferred_element_type=F32)
    mixed = _sigmoid(gt[:, 0:D_MODEL]) * o_a
    o_b = jnp.dot(yb_ref[0], wb_ref[...], preferred_element_type=F32)
    mixed = mixed + _sigmoid(gt[:, D_MODEL:2 * D_MODEL]) * o_b
    o_c = jnp.dot(yc_ref[0], wc_ref[...], preferred_element_type=F32)
    mixed = mixed + _sigmoid(gt[:, 2 * D_MODEL:3 * D_MODEL]) * o_c
    out = jnp.dot(mixed.astype(BF16), wo_ref[...], preferred_element_type=F32)
    o_ref[0] = x_ref[0] + g1_ref[0] * out


def _merge(ya, yb, yc, gates, x, g1, wa, wb, wc, wo, tm):
    B, S, D = x.shape
    tile = lambda n: pl.BlockSpec((1, tm, n), lambda b, i: (b, i, 0))
    const = lambda a: pl.BlockSpec(a.shape, lambda b, i: (0, 0))
    return pl.pallas_call(
        _merge_kernel,
        grid=(B, S // tm),
        in_specs=[tile(LRU_WIDTH), tile(RW_WIDTH), tile(ATT_WIDTH), tile(3 * D), tile(D),
                  pl.BlockSpec((1, 1, D), lambda b, i: (b, 0, 0)),
                  const(wa), const(wb), const(wc), const(wo)],
        out_specs=tile(D),
        out_shape=jax.ShapeDtypeStruct((B, S, D), F32),
        compiler_params=_params(("arbitrary", "arbitrary")),
        name="merge",
    )(ya, yb, yc, gates, x, g1, wa, wb, wc, wo)


FFN_CHUNKS = ((0, 1024), (1024, 1024), (2048, 768))


def _ffn_kernel(x_ref, sc_ref, sh_ref, g2_ref, wi_ref, wo_ref, o_ref):
    x = x_ref[0]
    ms = jnp.mean(x * x, axis=-1, keepdims=True)
    h = (x * lax.rsqrt(ms + RMS_EPS) * (1.0 + sc_ref[0]) + sh_ref[0]).astype(BF16)
    acc = jnp.zeros(x.shape, F32)
    for off, width in FFN_CHUNKS:
        gate = jnp.dot(h, wi_ref[:, off:off + width], preferred_element_type=F32)
        up = jnp.dot(h, wi_ref[:, D_FF + off:D_FF + off + width], preferred_element_type=F32)
        act = (gate * _sigmoid(gate) * up).astype(BF16)
        acc = acc + jnp.dot(act, wo_ref[off:off + width, :], preferred_element_type=F32)
    o_ref[0] = x + g2_ref[0] * acc


def _ffn(x, sc, sh, g2, wi, wo, tm):
    B, S, D = x.shape
    tile = pl.BlockSpec((1, tm, D), lambda b, i: (b, i, 0))
    vec = pl.BlockSpec((1, 1, D), lambda b, i: (b, 0, 0))
    return pl.pallas_call(
        _ffn_kernel,
        grid=(B, S // tm),
        in_specs=[tile, vec, vec, vec,
                  pl.BlockSpec(wi.shape, lambda b, i: (0, 0)),
                  pl.BlockSpec(wo.shape, lambda b, i: (0, 0))],
        out_specs=tile,
        out_shape=jax.ShapeDtypeStruct((B, S, D), F32),
        compiler_params=_params(("arbitrary", "arbitrary")),
        name="ffn",
    )(x, sc, sh, g2, wi, wo)


def _pad_rows(w, rows):
    return jnp.pad(w, ((0, rows - w.shape[0]), (0, 0)))


def _pad_cols(w, cols):
    return jnp.pad(w, ((0, 0), (0, cols - w.shape[1])))


def _block_diag_gates(wa, wx):
    per = LRU_SUPER // LRU_BLOCK
    n_super = LRU_WIDTH // LRU_SUPER
    on_diag = jnp.eye(per, dtype=bool)[None, :, None, :, None]

    def block_diag(w):
        w = w.reshape(n_super, per, LRU_BLOCK, 1, LRU_BLOCK)
        return jnp.where(on_diag, w, 0.0).reshape(n_super, LRU_SUPER, LRU_SUPER)

    return jnp.concatenate([block_diag(wa), block_diag(wx)], axis=2).astype(BF16)


def _tile_rows(S, want):
    t = min(want, S)
    while S % t:
        t //= 2
    return t


def kernel(x, c, w_ada, b_ada, w_in, conv_w, conv_b, lru_wa, lru_ba, lru_wx, lru_bx, lru_lambda, w_lru_o, rw_mu, rw_w0, rw_w2, rw_a0, rw_a2, rw_g2, rw_kk, rw_ka, rw_rk, rw_ln_w, rw_ln_b, rw_v0, rw_v1, rw_v2, w_rw_o, q_gain, k_gain, sinks, w_att_o, w_out, w_ffn_in, w_ffn_out):
    B, S, D = x.shape
    depth = w_in.shape[0]
    assert D == D_MODEL and S % WINDOW == 0
    tm_mm = _tile_rows(S, 512)
    tm_seq = _tile_rows(S, 256)
    ts_scan = _tile_rows(S, 2 * RW_GROUP * RW_CHUNK)

    mod = _ada(c, w_ada, b_ada).reshape(depth, B, 6, 1, D)

    o_lru = 2 * LRU_WIDTH
    o_rw = o_lru + 3 * RW_WIDTH + RW_DECAY_LORA + RW_AAA_LORA + RW_GATE_LORA
    o_att = o_rw + ATT_WIDTH + 2 * KV_WIDTH

    v_first = None
    for i in range(depth):
        sh1, sc1, g1, sh2, sc2, g2 = (mod[i, :, j] for j in range(6))
        wi = w_in[i]
        w_lru = wi[:, :o_lru].astype(BF16)
        w_rw_cols = [_pad_cols(wi[:, o_lru:o_rw], RW_COLS_PAD)]
        if i > 0:
            w_rw_cols.append(_pad_cols(rw_v1[i - 1], RW_MV_PAD))
        w_rw = jnp.concatenate(w_rw_cols, axis=1).astype(BF16)
        w_att = wi[:, o_rw:o_att].astype(BF16)
        w_gate = wi[:, o_att:].astype(BF16)

        h = _norm_mod(x, sc1, sh1, tm_mm)
        p_lru = _mm(h, w_lru, tm_mm, F32, "proj_lru")
        p_rw = _mm(h, w_rw, tm_mm, F32, "proj_rw")
        p_att = _mm(h, w_att, tm_mm, F32, "proj_att")
        p_gate = _mm(h, w_gate, tm_mm, F32, "proj_gate")

        ya = _lru_branch(p_lru, conv_w[i], conv_b[i], _block_diag_gates(lru_wa[i], lru_wx[i]),
                         lru_ba[i], lru_bx[i], lru_lambda[i], tm_seq)

        mu = _pad_cols(rw_mu[i].reshape(1, -1), RW_COLS_PAD)
        w2p = _pad_rows(rw_w2[i], LANES).astype(BF16)
        a2p = jnp.concatenate([jnp.zeros_like(rw_a2[i]), rw_a2[i]], axis=0).astype(BF16)
        g2p = _pad_rows(rw_g2[i], RW_GATE_PAD).astype(BF16)
        mix = None
        if i > 0:
            mix = (v_first, rw_v0[i - 1].reshape(1, -1), _pad_rows(rw_v2[i - 1], RW_MV_PAD).astype(BF16))
        r, lw, k, v, a, g = _rw_prep(p_rw, mu, rw_w0[i].reshape(1, -1), rw_a0[i].reshape(1, -1),
                                     w2p, a2p, g2p, tm_seq, mix)
        if i == 0:
            v_first = v
        yb = _rw_scan(r, lw, k, v, a, g, rw_kk[i].reshape(1, -1), rw_ka[i].reshape(1, -1),
                      rw_rk[i].reshape(1, -1), rw_ln_w[i].reshape(1, -1), rw_ln_b[i].reshape(1, -1), ts_scan)

        sinks_b = jnp.broadcast_to(sinks[i].reshape(-1, 1), (ATT_HEADS, LANES))
        yc = _attention(p_att, q_gain[i], k_gain[i], sinks_b)

        x = _merge(ya, yb, yc, p_gate, x, g1, w_lru_o[i].astype(BF16), w_rw_o[i].astype(BF16),
                   w_att_o[i].astype(BF16), w_out[i].astype(BF16), tm_mm)
        x = _ffn(x, sc2, sh2, g2, w_ffn_in[i].astype(BF16), w_ffn_out[i].astype(BF16), tm_mm)
    return x
```

```python
import functools

import jax
import jax.numpy as jnp
from jax import lax
from jax.experimental import pallas as pl
from jax.experimental.pallas import tpu as pltpu

D_MODEL = 1024
LRU_WIDTH = 1280
LRU_BLOCKS = 16
LRU_BLOCK = 80
LRU_SUPER = 640
CONV_WIDTH = 4
LRU_C = 8.0
RW_HEAD = 64
RW_WIDTH = 1024
RW_DECAY_LORA = 64
RW_AAA_LORA = 64
RW_MV_LORA = 32
RW_GATE_LORA = 160
RW_LN_EPS = 64e-5
ATT_HEAD = 64
ATT_HEADS = 16
ATT_KV_HEADS = 4
ATT_GROUP = 4
ATT_WIDTH = 1024
KV_WIDTH = 256
WINDOW = 128
D_FF = 2816
RMS_EPS = 1e-6
QK_EPS = 1e-6
NEG_INF = -1e30

LANES = 128
SUBLANES = 8
VMEM_LIMIT_BYTES = 56 * 1024 * 1024

RW_LORA_OFF = 3 * RW_WIDTH
RW_GATE_OFF = RW_LORA_OFF + LANES
RW_GATE_PAD = 2 * LANES
RW_COLS_PAD = RW_GATE_OFF + RW_GATE_PAD
RW_MV_PAD = LANES

MM_N_CHUNK = 512
RW_CHUNK = 64
RW_GROUP = 8

F32 = jnp.float32
BF16 = jnp.bfloat16
HIGHEST = lax.Precision.HIGHEST


def _sigmoid(x):
    return 0.5 * jnp.tanh(0.5 * x) + 0.5


def _softplus(z):
    return jnp.maximum(z, 0.0) + jnp.log(1.0 + jnp.exp(-jnp.abs(z)))


def _params(sem):
    return pltpu.CompilerParams(dimension_semantics=sem, vmem_limit_bytes=VMEM_LIMIT_BYTES)


def _ada_kernel(c_ref, w_ref, b_ref, o_ref):
    c = c_ref[...]
    act = (c * _sigmoid(c)).astype(BF16)
    o_ref[0] = jnp.dot(act, w_ref[0].astype(BF16), preferred_element_type=F32) + b_ref[0]


def _ada(c, w_ada, b_ada):
    L, D, N = w_ada.shape
    B = c.shape[0]
    tn = D_MODEL
    return pl.pallas_call(
        _ada_kernel,
        grid=(L, N // tn),
        in_specs=[
            pl.BlockSpec((B, D), lambda l, j: (0, 0)),
            pl.BlockSpec((1, D, tn), lambda l, j: (l, 0, j)),
            pl.BlockSpec((1, 1, tn), lambda l, j: (l, 0, j)),
        ],
        out_specs=pl.BlockSpec((1, B, tn), lambda l, j: (l, 0, j)),
        out_shape=jax.ShapeDtypeStruct((L, B, N), F32),
        compiler_params=_params(("arbitrary", "arbitrary")),
        name="ada_mod",
    )(c, w_ada, b_ada.reshape(L, 1, N))


def _norm_mod_kernel(x_ref, sc_ref, sh_ref, o_ref):
    x = x_ref[0]
    ms = jnp.mean(x * x, axis=-1, keepdims=True)
    h = x * lax.rsqrt(ms + RMS_EPS) * (1.0 + sc_ref[0]) + sh_ref[0]
    o_ref[0] = h.astype(o_ref.dtype)


def _norm_mod(x, sc, sh, tm):
    B, S, D = x.shape
    vec = pl.BlockSpec((1, 1, D), lambda b, i: (b, 0, 0))
    return pl.pallas_call(
        _norm_mod_kernel,
        grid=(B, S // tm),
        in_specs=[pl.BlockSpec((1, tm, D), lambda b, i: (b, i, 0)), vec, vec],
        out_specs=pl.BlockSpec((1, tm, D), lambda b, i: (b, i, 0)),
        out_shape=jax.ShapeDtypeStruct((B, S, D), BF16),
        compiler_params=_params(("arbitrary", "arbitrary")),
        name="norm_mod",
    )(x, sc, sh)


def _mm_kernel(x_ref, w_ref, o_ref, *, n_chunk):
    x = x_ref[0]
    n = o_ref.shape[-1]
    for j in range(0, n, n_chunk):
        w = min(n_chunk, n - j)
        o_ref[0, :, j:j + w] = jnp.dot(x, w_ref[:, j:j + w], preferred_element_type=F32).astype(o_ref.dtype)


def _mm(x, w, tm, out_dtype, name):
    B, S, K = x.shape
    N = w.shape[1]
    return pl.pallas_call(
        functools.partial(_mm_kernel, n_chunk=MM_N_CHUNK),
        grid=(B, S // tm),
        in_specs=[
            pl.BlockSpec((1, tm, K), lambda b, i: (b, i, 0)),
            pl.BlockSpec((K, N), lambda b, i: (0, 0)),
        ],
        out_specs=pl.BlockSpec((1, tm, N), lambda b, i: (b, i, 0)),
        out_shape=jax.ShapeDtypeStruct((B, S, N), out_dtype),
        compiler_params=_params(("arbitrary", "arbitrary")),
        name=name,
    )(x, w)


def _lru_kernel(hin_ref, wp_ref, cw_ref, cb_ref, wbd_ref, ba_ref, bx_ref, lam_ref, o_ref, xbuf_ref, h_ref, *, tm):
    @pl.when(pl.program_id(1) == 0)
    def _():
        xbuf_ref[...] = jnp.zeros_like(xbuf_ref)
        h_ref[...] = jnp.zeros_like(h_ref)

    hin = hin_ref[0]
    x = jnp.dot(hin, wp_ref[:, :LRU_WIDTH], preferred_element_type=F32)
    y = jnp.dot(hin, wp_ref[:, LRU_WIDTH:], preferred_element_type=F32)
    groups = tm // SUBLANES
    xg = jnp.concatenate([xbuf_ref[...], x], axis=0).reshape(groups + 1, SUBLANES, LRU_WIDTH)
    xbuf_ref[...] = x[tm - SUBLANES:]
    sub = lax.broadcasted_iota(jnp.int32, (groups, SUBLANES, LRU_WIDTH), 1)
    cw = cw_ref[...]
    xa = cb_ref[...] + x * cw[CONV_WIDTH - 1:CONV_WIDTH]
    for j in range(1, CONV_WIDTH):
        rot = pltpu.roll(xg, j, axis=1)
        delayed = jnp.where(sub >= j, rot[1:], rot[:-1]).reshape(tm, LRU_WIDTH)
        xa = xa + delayed * cw[CONV_WIDTH - 1 - j:CONV_WIDTH - j]

    xb = xa.astype(BF16)
    pre = [jnp.dot(xb[:, s * LRU_SUPER:(s + 1) * LRU_SUPER], wbd_ref[s], preferred_element_type=F32)
           for s in range(LRU_WIDTH // LRU_SUPER)]
    pre_a = jnp.concatenate([p[:, :LRU_SUPER] for p in pre], axis=-1)
    pre_x = jnp.concatenate([p[:, LRU_SUPER:] for p in pre], axis=-1)
    gate_a = _sigmoid(pre_a + ba_ref[...])
    gate_x = _sigmoid(pre_x + bx_ref[...])
    log_a = -LRU_C * gate_a * _softplus(-lam_ref[...])
    a = jnp.exp(log_a)
    z = 1.0 - a * a
    u = xa * gate_x * (z * lax.rsqrt(jnp.maximum(z, 1e-30)))

    a = a.reshape(groups, SUBLANES, LRU_WIDTH)
    u = u.reshape(groups, SUBLANES, LRU_WIDTH)
    d = 1
    while d < SUBLANES:
        keep = sub >= d
        u = jnp.where(keep, a * pltpu.roll(u, d, axis=1) + u, u)
        a = jnp.where(keep, a * pltpu.roll(a, d, axis=1), a)
        d *= 2
    carry = h_ref[0:1, :]
    hs = []
    for g in range(groups):
        hs.append(a[g] * carry + u[g])
        carry = hs[-1][SUBLANES - 1:SUBLANES, :]
    h = jnp.concatenate(hs, axis=0)
    h_ref[...] = jnp.broadcast_to(carry, h_ref.shape)

    gelu = 0.5 * y * (1.0 + jnp.tanh(0.7978845608028654 * (y + 0.044715 * (y * y * y))))
    o_ref[0] = (h * gelu).astype(o_ref.dtype)


def _lru_branch(h, w_lru, cw, cb, wbd, ba, bx, lam, tm):
    B, S, D = h.shape
    row = lambda n: pl.BlockSpec((1, n), lambda b, i: (0, 0))
    return pl.pallas_call(
        functools.partial(_lru_kernel, tm=tm),
        grid=(B, S // tm),
        in_specs=[
            pl.BlockSpec((1, tm, D), lambda b, i: (b, i, 0)),
            pl.BlockSpec((D, 2 * LRU_WIDTH), lambda b, i: (0, 0)),
            pl.BlockSpec((CONV_WIDTH, LRU_WIDTH), lambda b, i: (0, 0)),
            row(LRU_WIDTH),
            pl.BlockSpec((LRU_WIDTH // LRU_SUPER, LRU_SUPER, 2 * LRU_SUPER), lambda b, i: (0, 0, 0)),
            row(LRU_WIDTH), row(LRU_WIDTH), row(LRU_WIDTH),
        ],
        out_specs=pl.BlockSpec((1, tm, LRU_WIDTH), lambda b, i: (b, i, 0)),
        out_shape=jax.ShapeDtypeStruct((B, S, LRU_WIDTH), BF16),
        scratch_shapes=[pltpu.VMEM((SUBLANES, LRU_WIDTH), F32), pltpu.VMEM((SUBLANES, LRU_WIDTH), F32)],
        compiler_params=_params(("arbitrary", "arbitrary")),
        name="lru_branch",
    )(h, w_lru, cw, cb.reshape(1, -1), wbd, ba.reshape(1, -1), bx.reshape(1, -1), lam.reshape(1, -1))


def _rw_prep_kernel(*refs, tm, has_mix):
    if has_mix:
        (h_ref, wp_ref, mu_ref, w0_ref, a0_ref, w2_ref, a2_ref, g2_ref, vf_ref, v0_ref, v2_ref,
         r_ref, lw_ref, k_ref, v_ref, a_ref, g_ref, prev_ref, p_ref) = refs
    else:
        (h_ref, wp_ref, mu_ref, w0_ref, a0_ref, w2_ref, a2_ref, g2_ref,
         r_ref, lw_ref, k_ref, v_ref, a_ref, g_ref, prev_ref, p_ref) = refs

    @pl.when(pl.program_id(1) == 0)
    def _():
        prev_ref[...] = jnp.zeros_like(prev_ref)

    h = h_ref[0]
    ncol = p_ref.shape[-1]
    for j in range(0, ncol, MM_N_CHUNK):
        w = min(MM_N_CHUNK, ncol - j)
        p_ref[:, j:j + w] = jnp.dot(h, wp_ref[:, j:j + w], preferred_element_type=F32)

    x = p_ref[:, :RW_COLS_PAD]
    groups = tm // SUBLANES
    rot = pltpu.roll(jnp.concatenate([prev_ref[...], x], axis=0).reshape(groups + 1, SUBLANES, RW_COLS_PAD),
                     1, axis=1)
    sub = lax.broadcasted_iota(jnp.int32, (groups, SUBLANES, RW_COLS_PAD), 1)
    xs = jnp.where(sub >= 1, rot[1:], rot[:-1]).reshape(tm, RW_COLS_PAD)
    prev_ref[...] = x[tm - SUBLANES:]
    p = x + (xs - x) * mu_ref[...]

    r_ref[0] = p[:, 0:RW_WIDTH]
    k_ref[0] = p[:, RW_WIDTH:2 * RW_WIDTH]
    v = p[:, 2 * RW_WIDTH:3 * RW_WIDTH]
    lora = p[:, RW_LORA_OFF:RW_GATE_OFF]
    gl = p[:, RW_GATE_OFF:RW_COLS_PAD]

    w = w0_ref[...] + jnp.dot(jnp.tanh(lora).astype(BF16), w2_ref[...], preferred_element_type=F32)
    w = -_softplus(-w) - 0.5
    lw_ref[0] = -jnp.exp(w)
    a_ref[0] = _sigmoid(a0_ref[...] + jnp.dot(lora.astype(BF16), a2_ref[...], preferred_element_type=F32))
    g_ref[0] = jnp.dot(_sigmoid(gl).astype(BF16), g2_ref[...], preferred_element_type=F32)
    if has_mix:
        hv = p_ref[:, RW_COLS_PAD:RW_COLS_PAD + RW_MV_PAD]
        mix = _sigmoid(v0_ref[...] + jnp.dot(hv.astype(BF16), v2_ref[...], preferred_element_type=F32))
        v = v + (vf_ref[0] - v) * mix
    v_ref[0] = v


def _rw_prep(h, w_rw, mu, w0, a0, w2p, a2p, g2p, tm, mix=None):
    B, S, D = h.shape
    ncol = w_rw.shape[1]
    has_mix = mix is not None
    const = lambda shape: pl.BlockSpec(shape, lambda b, i: (0,) * len(shape))
    tile = lambda n: pl.BlockSpec((1, tm, n), lambda b, i: (b, i, 0))
    in_specs = [tile(D), const((D, ncol)), const((1, RW_COLS_PAD)), const((1, RW_WIDTH)), const((1, RW_WIDTH)),
                const((LANES, RW_WIDTH)), const((LANES, RW_WIDTH)), const((RW_GATE_PAD, RW_WIDTH))]
    args = [h, w_rw, mu, w0, a0, w2p, a2p, g2p]
    if has_mix:
        v_first, v0, v2p = mix
        in_specs += [tile(RW_WIDTH), const((1, RW_WIDTH)), const((RW_MV_PAD, RW_WIDTH))]
        args += [v_first, v0, v2p]
    out = jax.ShapeDtypeStruct((B, S, RW_WIDTH), F32)
    return pl.pallas_call(
        functools.partial(_rw_prep_kernel, tm=tm, has_mix=has_mix),
        grid=(B, S // tm),
        in_specs=in_specs,
        out_specs=[tile(RW_WIDTH)] * 6,
        out_shape=[out] * 6,
        scratch_shapes=[pltpu.VMEM((SUBLANES, RW_COLS_PAD), F32), pltpu.VMEM((tm, ncol), F32)],
        compiler_params=_params(("arbitrary", "arbitrary")),
        name="rw_prep",
    )(*args)


def _dot(a, b):
    return jnp.dot(a, b, preferred_element_type=F32)


def _dot_nt(a, b):
    return lax.dot_general(a, b, (((1,), (1,)), ((), ())), preferred_element_type=F32)


def _split(x):
    hi = x.astype(BF16)
    return hi, (x - hi.astype(F32)).astype(BF16)


def _seg_dot(x, seg2):
    hi, lo = _split(x)
    return _dot(jnp.concatenate([hi, lo], axis=1), seg2)


def _rw_scan_kernel(r_ref, lw_ref, k_ref, v_ref, a_ref, g_ref, kk_ref, ka_ref, rk_ref, lnw_ref, lnb_ref,
                    o_ref, m_ref, y_ref, *, ts):
    C = RW_CHUNK
    N = 2 * C

    @pl.when(pl.program_id(2) == 0)
    def _():
        m_ref[...] = jnp.zeros_like(m_ref)

    ri = lax.broadcasted_iota(jnp.int32, (N, N), 0)
    ci = lax.broadcasted_iota(jnp.int32, (N, N), 1)
    same = (ri >= C) == (ci >= C)
    strict = same & (ri > ci)
    incl = same & (ri >= ci)
    eye = (ri == ci).astype(F32)
    seg = jnp.concatenate([same.astype(BF16)] * 2, axis=0)
    tri2 = jnp.concatenate([incl.astype(BF16)] * 2, axis=1)
    lane_lo = lax.broadcasted_iota(jnp.int32, (C, N), 1) < RW_HEAD

    def stack_masked(x):
        return jnp.concatenate([jnp.where(lane_lo, x, 0.0), jnp.where(lane_lo, 0.0, x)], axis=0)

    def stack(x):
        return jnp.concatenate([x, x], axis=0)

    kk_p = kk_ref[...]
    ka_p = ka_ref[...]
    rows = lambda x, c: x[c * C:(c + 1) * C]

    def phase1(first):
        chunks = range(RW_GROUP)
        gs = slice(first * C, (first + RW_GROUP) * C)
        r = r_ref[0, gs, :]
        lw = lw_ref[0, gs, :]
        v = v_ref[0, gs, :]
        a = a_ref[0, gs, :]
        kkr = k_ref[0, gs, :] * kk_p
        kk = kkr * lax.rsqrt(jnp.maximum(_seg_dot(kkr * kkr, seg), 1e-24))
        k = k_ref[0, gs, :] * (1.0 + (a - 1.0) * ka_p)
        b_s = kk * a

        lw_hi, lw_lo = _split(lw)
        wide = lambda x: jnp.concatenate([stack(rows(x, c)) for c in chunks], axis=1)
        cum_w = _dot(tri2, jnp.concatenate([wide(lw_hi), wide(lw_lo)], axis=0))
        cum = jnp.concatenate([cum_w[:C, c * N:(c + 1) * N] for c in chunks], axis=0)
        last = [cum_w[C - 1:C, c * N:(c + 1) * N] for c in chunks]
        last_b = jnp.concatenate([jnp.broadcast_to(l, (C, N)) for l in last], axis=0)
        e_inv = jnp.exp(-cum)
        e_out = jnp.exp(last_b - cum)
        a_t = -kk * jnp.exp(cum - lw)
        r_t = r * jnp.exp(cum)
        b_in = b_s * e_inv
        k_in = k * e_inv
        b_out = b_s * e_out
        k_out = k * e_out
        yield None

        a_st = [stack_masked(rows(a_t, c)).astype(BF16) for c in chunks]
        r_st = [stack_masked(rows(r_t, c)).astype(BF16) for c in chunks]
        v_st = [stack_masked(rows(v, c)).astype(BF16) for c in chunks]
        rhs = [jnp.concatenate([stack(rows(b_in, c)), stack(rows(k_in, c))], axis=0).astype(BF16) for c in chunks]
        gram = [_dot_nt(jnp.concatenate([a_st[c], r_st[c]], axis=0), rhs[c]) for c in chunks]
        ab = [jnp.where(strict, g[:N, :N], 0.0) for g in gram]
        ak = [jnp.where(strict, g[:N, N:], 0.0).astype(BF16) for g in gram]
        rb = [jnp.where(incl, g[N:, :N], 0.0).astype(BF16) for g in gram]
        rk = [jnp.where(incl, g[N:, N:], 0.0).astype(BF16) for g in gram]
        yield None

        s_inv = [eye + x for x in ab]
        q = [x.astype(BF16) for x in ab]
        q = [_dot(x, x) for x in q]
        yield None
        for _ in range(4):
            q_b = [x.astype(BF16) for x in q]
            prod = [_dot(q_b[c], jnp.concatenate([q_b[c], s_inv[c].astype(BF16)], axis=1)) for c in chunks]
            q = [p[:, :N] for p in prod]
            s_inv = [s + p[:, N:] for s, p in zip(s_inv, prod)]
            yield None
        t_b = [(s + _dot(x.astype(BF16), s.astype(BF16))).astype(BF16) for s, x in zip(s_inv, q)]
        yield None

        akv = [_dot(ak[c], v_st[c]).astype(BF16) for c in chunks]
        yield None
        uw = [_dot(t_b[c], jnp.concatenate([akv[c], a_st[c]], axis=1)).astype(BF16) for c in chunks]
        bh_t = [stack_masked(rows(b_out, c)).T.astype(BF16) for c in chunks]
        kh_t = [stack_masked(rows(k_out, c)).T.astype(BF16) for c in chunks]
        yield None
        zero = jnp.zeros((N, N), BF16)
        left = [jnp.concatenate([jnp.concatenate([bh_t[c], kh_t[c]], axis=1),
                                 jnp.concatenate([rb[c], rk[c]], axis=1)], axis=0) for c in chunks]
        right = [jnp.concatenate([jnp.concatenate([uw[c][:, N:], uw[c][:, :N]], axis=1),
                                  jnp.concatenate([zero, v_st[c]], axis=1)], axis=0) for c in chunks]
        both = [_dot(left[c], right[c]) for c in chunks]
        lin = [jnp.concatenate([both[c][:N, :N], stack_masked(rows(r_t, c)) + both[c][N:, :N]],
                               axis=0).astype(BF16) for c in chunks]
        decay_col = [jnp.broadcast_to(jnp.exp(l), (N, N)).T for l in last]
        yield [(first + c, lin[c], both[c][:N, N:], both[c][N:, N:], decay_col[c]) for c in chunks]

    state = [m_ref[...]]

    def phase2_step(item):
        c, lin, m_add, y_add, decay_col = item
        m = state[0]
        x = _dot(lin, m.astype(BF16))
        y_st = x[N:] + y_add
        state[0] = decay_col * m + x[:N] + m_add
        y_ref[c * C:(c + 1) * C, :] = y_st[:C] + y_st[C:]

    pending = []
    for first in range(0, ts // C, RW_GROUP):
        maps = None
        for item in phase1(first):
            if item is not None:
                maps = item
            elif pending:
                phase2_step(pending.pop(0))
        while pending:
            phase2_step(pending.pop(0))
        pending = list(maps)
    while pending:
        phase2_step(pending.pop(0))
    m_ref[...] = state[0]

    y = y_ref[...]
    a = a_ref[0]
    k = k_ref[0] * (1.0 + (a - 1.0) * ka_p)
    mean = _seg_dot(y, seg) * (1.0 / RW_HEAD)
    yc = y - mean
    var = _seg_dot(yc * yc, seg) * (1.0 / RW_HEAD)
    yn = yc * lax.rsqrt(var + RW_LN_EPS) * lnw_ref[...] + lnb_ref[...]
    bonus = _seg_dot(r_ref[0] * k * rk_ref[...], seg) * v_ref[0]
    o_ref[0] = ((yn + bonus) * g_ref[0]).astype(o_ref.dtype)


def _rw_scan(r, lw, k, v, a, g, kk_p, ka_p, rk_p, lnw, lnb, ts):
    B, S, W = r.shape
    tile = pl.BlockSpec((1, ts, LANES), lambda b, h, i: (b, i, h))
    vec = pl.BlockSpec((1, LANES), lambda b, h, i: (0, h))
    return pl.pallas_call(
        functools.partial(_rw_scan_kernel, ts=ts),
        grid=(B, W // LANES, S // ts),
        in_specs=[tile] * 6 + [vec] * 5,
        out_specs=tile,
        out_shape=jax.ShapeDtypeStruct((B, S, W), BF16),
        scratch_shapes=[pltpu.VMEM((LANES, LANES), F32), pltpu.VMEM((ts, LANES), F32)],
        compiler_params=_params(("arbitrary", "arbitrary", "arbitrary")),
        name="rw_scan",
    )(r, lw, k, v, a, g, kk_p, ka_p, rk_p, lnw, lnb)


def _attn_kernel(q_ref, k_ref, v_ref, qg_ref, kg_ref, sink_ref, o_ref, kpad_ref, vpad_ref):
    W = WINDOW
    n = pl.program_id(1)

    band_lo = lax.broadcasted_iota(jnp.int32, (2 * W, LANES), 1) < ATT_HEAD
    ones_even = jnp.where(band_lo, 1.0, 0.0).astype(BF16)
    ones_odd = jnp.where(band_lo, 0.0, 1.0).astype(BF16)

    @pl.when(n == 0)
    def _():
        kpad_ref[...] = jnp.zeros_like(kpad_ref)
        for t in range(2 * ATT_KV_HEADS):
            vpad_ref[t, :, 0:LANES] = jnp.zeros((2 * W, LANES), BF16)
            vpad_ref[t, :, LANES:2 * LANES] = ones_odd if t % 2 else ones_even

    kpad_ref[:, 0:W, :] = kpad_ref[:, W:2 * W, :]
    vpad_ref[:, 0:W, 0:LANES] = vpad_ref[:, W:2 * W, 0:LANES]

    ri = lax.broadcasted_iota(jnp.int32, (LANES, LANES), 0)
    ci = lax.broadcasted_iota(jnp.int32, (LANES, LANES), 1)
    seg = jnp.concatenate([((ri >= ATT_HEAD) == (ci >= ATT_HEAD)).astype(BF16)] * 2, axis=0)
    lo = lax.broadcasted_iota(jnp.int32, (W, LANES), 1) < ATT_HEAD

    def head_rms(t, gain):
        ms = _seg_dot(t * t, seg) * (1.0 / ATT_HEAD)
        return t * lax.rsqrt(ms + QK_EPS) * gain

    kg = kg_ref[...]
    for c in range(KV_WIDTH // LANES):
        cs = slice(c * LANES, (c + 1) * LANES)
        kn = head_rms(k_ref[0, :, cs], kg)
        vv = v_ref[0, :, cs]
        for t, src in ((kpad_ref, kn), (vpad_ref, vv)):
            left = jnp.where(lo, src, 0.0)
            right = jnp.where(lo, 0.0, src)
            t[4 * c + 0, W:2 * W, 0:LANES] = left.astype(BF16)
            t[4 * c + 1, W:2 * W, 0:LANES] = pltpu.roll(left, ATT_HEAD, axis=1).astype(BF16)
            t[4 * c + 2, W:2 * W, 0:LANES] = pltpu.roll(right, ATT_HEAD, axis=1).astype(BF16)
            t[4 * c + 3, W:2 * W, 0:LANES] = right.astype(BF16)

    qi = lax.broadcasted_iota(jnp.int32, (W, 2 * W), 0) + W
    kj = lax.broadcasted_iota(jnp.int32, (W, 2 * W), 1)
    mask = (kj <= qi) & (qi - kj < W) & ((n * W - W + kj) >= 0)
    mask_add = jnp.where(mask, 0.0, NEG_INF)

    qg = qg_ref[...] * (ATT_HEAD ** -0.5)
    sink = [sink_ref[hd:hd + 1, 0:1] for hd in range(ATT_HEADS)]
    slot = [2 * (hd // ATT_GROUP) + hd % 2 for hd in range(ATT_HEADS)]

    def scores(st, heads):
        pairs = sorted({hd // 2 for hd in heads})
        q = {p: head_rms(q_ref[0, :, p * LANES:(p + 1) * LANES], qg).astype(BF16) for p in pairs}
        st["s"] = {hd: _dot_nt(q[hd // 2], kpad_ref[slot[hd]]) + mask_add for hd in heads}

    def row_max(st, heads):
        st["m"] = {hd: jnp.broadcast_to(jnp.maximum(jnp.max(st["s"][hd], axis=-1, keepdims=True), sink[hd]),
                                        (W, LANES)) for hd in heads}

    def exponent(st, heads):
        st["e"] = {hd: jnp.exp(st["s"][hd] - jnp.concatenate([st["m"][hd]] * 2, axis=-1)).astype(BF16)
                   for hd in heads}
        st["sink_add"] = {hd: jnp.exp(sink[hd] - st["m"][hd]) for hd in heads}

    def weighted(st, heads):
        st["pv"] = {hd: _dot(st["e"][hd], vpad_ref[slot[hd]]) for hd in heads}

    def finish(st, heads):
        for p in sorted({hd // 2 for hd in heads}):
            both = st["pv"][2 * p] + st["pv"][2 * p + 1]
            den = both[:, LANES:] + jnp.where(lo, st["sink_add"][2 * p], st["sink_add"][2 * p + 1])
            o_ref[0, :, p * LANES:(p + 1) * LANES] = (both[:, :LANES] / den).astype(o_ref.dtype)

    stages = (scores, row_max, exponent, weighted, finish)
    groups = [list(range(j * ATT_GROUP, (j + 1) * ATT_GROUP)) for j in range(ATT_KV_HEADS)]
    states = [{} for _ in groups]
    for t in range(len(groups) + len(stages) - 1):
        for gi, heads in enumerate(groups):
            if 0 <= t - gi < len(stages):
                stages[t - gi](states[gi], heads)


def _attention(p_att, q_gain, k_gain, sinks_b):
    B, S, _ = p_att.shape
    nq = ATT_WIDTH // KV_WIDTH
    pair = lambda g: jnp.concatenate([g, g]).reshape(1, LANES)
    return pl.pallas_call(
        _attn_kernel,
        grid=(B, S // WINDOW),
        in_specs=[
            pl.BlockSpec((1, WINDOW, ATT_WIDTH), lambda b, n: (b, n, 0)),
            pl.BlockSpec((1, WINDOW, KV_WIDTH), lambda b, n: (b, n, nq)),
            pl.BlockSpec((1, WINDOW, KV_WIDTH), lambda b, n: (b, n, nq + 1)),
            pl.BlockSpec((1, LANES), lambda b, n: (0, 0)),
            pl.BlockSpec((1, LANES), lambda b, n: (0, 0)),
            pl.BlockSpec((ATT_HEADS, LANES), lambda b, n: (0, 0)),
        ],
        out_specs=pl.BlockSpec((1, WINDOW, ATT_WIDTH), lambda b, n: (b, n, 0)),
        out_shape=jax.ShapeDtypeStruct((B, S, ATT_WIDTH), BF16),
        scratch_shapes=[pltpu.VMEM((2 * ATT_KV_HEADS, 2 * WINDOW, LANES), BF16),
                        pltpu.VMEM((2 * ATT_KV_HEADS, 2 * WINDOW, 2 * LANES), BF16)],
        compiler_params=_params(("arbitrary", "arbitrary")),
        name="swa_attention",
    )(p_att, p_att, p_att, pair(q_gain), pair(k_gain), sinks_b)


def _merge_kernel(ya_ref, yb_ref, yc_ref, gt_ref, x_ref, g1_ref, wa_ref, wb_ref, wc_ref, wo_ref, o_ref):
    gt = gt_ref[0]
    o_a = jnp.dot(ya_ref[0], wa_ref[...], preferred_element_type=F32)
    mixed = _sigmoid(gt[:, 0:D_MODEL]) * o_a
    o_b = jnp.dot(yb_ref[0], wb_ref[...], preferred_element_type=F32)
    mixed = mixed + _sigmoid(gt[:, D_MODEL:2 * D_MODEL]) * o_b
    o_c = jnp.dot(yc_ref[0], wc_ref[...], preferred_element_type=F32)
    mixed = mixed + _sigmoid(gt[:, 2 * D_MODEL:3 * D_MODEL]) * o_c
    out = jnp.dot(mixed.astype(BF16), wo_ref[...], preferred_element_type=F32)
    o_ref[0] = x_ref[0] + g1_ref[0] * out


def _merge(ya, yb, yc, gates, x, g1, wa, wb, wc, wo, tm):
    B, S, D = x.shape
    tile = lambda n: pl.BlockSpec((1, tm, n), lambda b, i: (b, i, 0))
    const = lambda a: pl.BlockSpec(a.shape, lambda b, i: (0, 0))
    return pl.pallas_call(
        _merge_kernel,
        grid=(B, S // tm),
        in_specs=[tile(LRU_WIDTH), tile(RW_WIDTH), tile(ATT_WIDTH), tile(3 * D), tile(D),
                  pl.BlockSpec((1, 1, D), lambda b, i: (b, 0, 0)),
                  const(wa), const(wb), const(wc), const(wo)],
        out_specs=tile(D),
        out_shape=jax.ShapeDtypeStruct((B, S, D), F32),
        compiler_params=_params(("arbitrary", "arbitrary")),
        name="merge",
    )(ya, yb, yc, gates, x, g1, wa, wb, wc, wo)


FFN_CHUNKS = ((0, 1024), (1024, 1024), (2048, 768))


def _ffn_kernel(x_ref, sc_ref, sh_ref, g2_ref, wi_ref, wo_ref, o_ref):
    x = x_ref[0]
    ms = jnp.mean(x * x, axis=-1, keepdims=True)
    h = (x * lax.rsqrt(ms + RMS_EPS) * (1.0 + sc_ref[0]) + sh_ref[0]).astype(BF16)
    acc = jnp.zeros(x.shape, F32)
    for off, width in FFN_CHUNKS:
        gate = jnp.dot(h, wi_ref[:, off:off + width], preferred_element_type=F32)
        up = jnp.dot(h, wi_ref[:, D_FF + off:D_FF + off + width], preferred_element_type=F32)
        act = (gate * _sigmoid(gate) * up).astype(BF16)
        acc = acc + jnp.dot(act, wo_ref[off:off + width, :], preferred_element_type=F32)
    o_ref[0] = x + g2_ref[0] * acc


def _ffn(x, sc, sh, g2, wi, wo, tm):
    B, S, D = x.shape
    tile = pl.BlockSpec((1, tm, D), lambda b, i: (b, i, 0))
    vec = pl.BlockSpec((1, 1, D), lambda b, i: (b, 0, 0))
    return pl.pallas_call(
        _ffn_kernel,
        grid=(B, S // tm),
        in_specs=[tile, vec, vec, vec,
                  pl.BlockSpec(wi.shape, lambda b, i: (0, 0)),
                  pl.BlockSpec(wo.shape, lambda b, i: (0, 0))],
        out_specs=tile,
        out_shape=jax.ShapeDtypeStruct((B, S, D), F32),
        compiler_params=_params(("arbitrary", "arbitrary")),
        name="ffn",
    )(x, sc, sh, g2, wi, wo)


def _pad_rows(w, rows):
    return jnp.pad(w, ((0, rows - w.shape[0]), (0, 0)))


def _pad_cols(w, cols):
    return jnp.pad(w, ((0, 0), (0, cols - w.shape[1])))


def _block_diag_gates(wa, wx):
    per = LRU_SUPER // LRU_BLOCK
    n_super = LRU_WIDTH // LRU_SUPER
    on_diag = jnp.eye(per, dtype=bool)[None, :, None, :, None]

    def block_diag(w):
        w = w.reshape(n_super, per, LRU_BLOCK, 1, LRU_BLOCK)
        return jnp.where(on_diag, w, 0.0).reshape(n_super, LRU_SUPER, LRU_SUPER)

    return jnp.concatenate([block_diag(wa), block_diag(wx)], axis=2).astype(BF16)


def _tile_rows(S, want):
    t = min(want, S)
    while S % t:
        t //= 2
    return t


def kernel(x, c, w_ada, b_ada, w_in, conv_w, conv_b, lru_wa, lru_ba, lru_wx, lru_bx, lru_lambda, w_lru_o, rw_mu, rw_w0, rw_w2, rw_a0, rw_a2, rw_g2, rw_kk, rw_ka, rw_rk, rw_ln_w, rw_ln_b, rw_v0, rw_v1, rw_v2, w_rw_o, q_gain, k_gain, sinks, w_att_o, w_out, w_ffn_in, w_ffn_out):
    B, S, D = x.shape
    depth = w_in.shape[0]
    assert D == D_MODEL and S % WINDOW == 0
    tm_mm = _tile_rows(S, 512)
    tm_seq = _tile_rows(S, 256)
    ts_scan = _tile_rows(S, 2 * RW_GROUP * RW_CHUNK)

    mod = _ada(c, w_ada, b_ada).reshape(depth, B, 6, 1, D)

    o_lru = 2 * LRU_WIDTH
    o_rw = o_lru + 3 * RW_WIDTH + RW_DECAY_LORA + RW_AAA_LORA + RW_GATE_LORA
    o_att = o_rw + ATT_WIDTH + 2 * KV_WIDTH

    v_first = None
    for i in range(depth):
        sh1, sc1, g1, sh2, sc2, g2 = (mod[i, :, j] for j in range(6))
        wi = w_in[i]
        w_lru = wi[:, :o_lru].astype(BF16)
        w_rw_cols = [_pad_cols(wi[:, o_lru:o_rw], RW_COLS_PAD)]
        if i > 0:
            w_rw_cols.append(_pad_cols(rw_v1[i - 1], RW_MV_PAD))
        w_rw = jnp.concatenate(w_rw_cols, axis=1).astype(BF16)
        w_att = wi[:, o_rw:o_att].astype(BF16)
        w_gate = wi[:, o_att:].astype(BF16)

        h = _norm_mod(x, sc1, sh1, tm_mm)
        p_att = _mm(h, w_att, tm_mm, F32, "proj_att")
        p_gate = _mm(h, w_gate, tm_mm, F32, "proj_gate")

        ya = _lru_branch(h, w_lru, conv_w[i], conv_b[i], _block_diag_gates(lru_wa[i], lru_wx[i]),
                         lru_ba[i], lru_bx[i], lru_lambda[i], tm_seq)

        mu = _pad_cols(rw_mu[i].reshape(1, -1), RW_COLS_PAD)
        w2p = _pad_rows(rw_w2[i], LANES).astype(BF16)
        a2p = jnp.concatenate([jnp.zeros_like(rw_a2[i]), rw_a2[i]], axis=0).astype(BF16)
        g2p = _pad_rows(rw_g2[i], RW_GATE_PAD).astype(BF16)
        mix = None
        if i > 0:
            mix = (v_first, rw_v0[i - 1].reshape(1, -1), _pad_rows(rw_v2[i - 1], RW_MV_PAD).astype(BF16))
        r, lw, k, v, a, g = _rw_prep(h, w_rw, mu, rw_w0[i].reshape(1, -1), rw_a0[i].reshape(1, -1),
                                     w2p, a2p, g2p, tm_seq, mix)
        if i == 0:
            v_first = v
        yb = _rw_scan(r, lw, k, v, a, g, rw_kk[i].reshape(1, -1), rw_ka[i].reshape(1, -1),
                      rw_rk[i].reshape(1, -1), rw_ln_w[i].reshape(1, -1), rw_ln_b[i].reshape(1, -1), ts_scan)

        sinks_b = jnp.broadcast_to(sinks[i].reshape(-1, 1), (ATT_HEADS, LANES))
        yc = _attention(p_att, q_gain[i], k_gain[i], sinks_b)

        x = _merge(ya, yb, yc, p_gate, x, g1, w_lru_o[i].astype(BF16), w_rw_o[i].astype(BF16),
                   w_att_o[i].astype(BF16), w_out[i].astype(BF16), tm_mm)
        x = _ffn(x, sc2, sh2, g2, w_ffn_in[i].astype(BF16), w_ffn_out[i].astype(BF16), tm_mm)
    return x
```

```python
import functools

import jax
import jax.numpy as jnp
from jax import lax
from jax.experimental import pallas as pl
from jax.experimental.pallas import tpu as pltpu

D_MODEL = 1024
LRU_WIDTH = 1280
LRU_BLOCKS = 16
LRU_BLOCK = 80
LRU_SUPER = 640
CONV_WIDTH = 4
LRU_C = 8.0
RW_HEAD = 64
RW_WIDTH = 1024
RW_DECAY_LORA = 64
RW_AAA_LORA = 64
RW_MV_LORA = 32
RW_GATE_LORA = 160
RW_LN_EPS = 64e-5
ATT_HEAD = 64
ATT_HEADS = 16
ATT_KV_HEADS = 4
ATT_GROUP = 4
ATT_WIDTH = 1024
KV_WIDTH = 256
WINDOW = 128
D_FF = 2816
RMS_EPS = 1e-6
QK_EPS = 1e-6
NEG_INF = -1e30

LANES = 128
SUBLANES = 8
VMEM_LIMIT_BYTES = 56 * 1024 * 1024

RW_LORA_OFF = 3 * RW_WIDTH
RW_GATE_OFF = RW_LORA_OFF + LANES
RW_GATE_PAD = 2 * LANES
RW_COLS_PAD = RW_GATE_OFF + RW_GATE_PAD
RW_MV_PAD = LANES

MM_N_CHUNK = 512
RW_CHUNK = 64
RW_GROUP = 8
RW_GROUPS_PER_STEP = 4

F32 = jnp.float32
BF16 = jnp.bfloat16
HIGHEST = lax.Precision.HIGHEST


def _sigmoid(x):
    return 0.5 * jnp.tanh(0.5 * x) + 0.5


def _softplus(z):
    return jnp.maximum(z, 0.0) + jnp.log(1.0 + jnp.exp(-jnp.abs(z)))


def _params(sem):
    return pltpu.CompilerParams(dimension_semantics=sem, vmem_limit_bytes=VMEM_LIMIT_BYTES)


def _ada_kernel(c_ref, w_ref, b_ref, o_ref):
    c = c_ref[...]
    act = (c * _sigmoid(c)).astype(BF16)
    o_ref[0] = jnp.dot(act, w_ref[0].astype(BF16), preferred_element_type=F32) + b_ref[0]


def _ada(c, w_ada, b_ada):
    L, D, N = w_ada.shape
    B = c.shape[0]
    tn = D_MODEL
    return pl.pallas_call(
        _ada_kernel,
        grid=(L, N // tn),
        in_specs=[
            pl.BlockSpec((B, D), lambda l, j: (0, 0)),
            pl.BlockSpec((1, D, tn), lambda l, j: (l, 0, j)),
            pl.BlockSpec((1, 1, tn), lambda l, j: (l, 0, j)),
        ],
        out_specs=pl.BlockSpec((1, B, tn), lambda l, j: (l, 0, j)),
        out_shape=jax.ShapeDtypeStruct((L, B, N), F32),
        compiler_params=_params(("arbitrary", "arbitrary")),
        name="ada_mod",
    )(c, w_ada, b_ada.reshape(L, 1, N))


def _norm_mod_kernel(x_ref, sc_ref, sh_ref, o_ref):
    x = x_ref[0]
    ms = jnp.mean(x * x, axis=-1, keepdims=True)
    h = x * lax.rsqrt(ms + RMS_EPS) * (1.0 + sc_ref[0]) + sh_ref[0]
    o_ref[0] = h.astype(o_ref.dtype)


def _norm_mod(x, sc, sh, tm):
    B, S, D = x.shape
    vec = pl.BlockSpec((1, 1, D), lambda b, i: (b, 0, 0))
    return pl.pallas_call(
        _norm_mod_kernel,
        grid=(B, S // tm),
        in_specs=[pl.BlockSpec((1, tm, D), lambda b, i: (b, i, 0)), vec, vec],
        out_specs=pl.BlockSpec((1, tm, D), lambda b, i: (b, i, 0)),
        out_shape=jax.ShapeDtypeStruct((B, S, D), BF16),
        compiler_params=_params(("arbitrary", "arbitrary")),
        name="norm_mod",
    )(x, sc, sh)


def _mm_kernel(x_ref, w_ref, o_ref, *, n_chunk):
    x = x_ref[0]
    n = o_ref.shape[-1]
    for j in range(0, n, n_chunk):
        w = min(n_chunk, n - j)
        o_ref[0, :, j:j + w] = jnp.dot(x, w_ref[:, j:j + w], preferred_element_type=F32).astype(o_ref.dtype)


def _mm(x, w, tm, out_dtype, name):
    B, S, K = x.shape
    N = w.shape[1]
    return pl.pallas_call(
        functools.partial(_mm_kernel, n_chunk=MM_N_CHUNK),
        grid=(B, S // tm),
        in_specs=[
            pl.BlockSpec((1, tm, K), lambda b, i: (b, i, 0)),
            pl.BlockSpec((K, N), lambda b, i: (0, 0)),
        ],
        out_specs=pl.BlockSpec((1, tm, N), lambda b, i: (b, i, 0)),
        out_shape=jax.ShapeDtypeStruct((B, S, N), out_dtype),
        compiler_params=_params(("arbitrary", "arbitrary")),
        name=name,
    )(x, w)


def _lru_kernel(hin_ref, wp_ref, cw_ref, cb_ref, wbd_ref, ba_ref, bx_ref, lam_ref, o_ref, xbuf_ref, h_ref, *, tm):
    @pl.when(pl.program_id(1) == 0)
    def _():
        xbuf_ref[...] = jnp.zeros_like(xbuf_ref)
        h_ref[...] = jnp.zeros_like(h_ref)

    hin = hin_ref[0]
    x = jnp.dot(hin, wp_ref[:, :LRU_WIDTH], preferred_element_type=F32)
    groups = tm // SUBLANES
    xg = jnp.concatenate([xbuf_ref[...], x], axis=0).reshape(groups + 1, SUBLANES, LRU_WIDTH)
    xbuf_ref[...] = x[tm - SUBLANES:]
    sub = lax.broadcasted_iota(jnp.int32, (groups, SUBLANES, LRU_WIDTH), 1)
    cw = cw_ref[...]
    xa = cb_ref[...] + x * cw[CONV_WIDTH - 1:CONV_WIDTH]
    for j in range(1, CONV_WIDTH):
        rot = pltpu.roll(xg, j, axis=1)
        delayed = jnp.where(sub >= j, rot[1:], rot[:-1]).reshape(tm, LRU_WIDTH)
        xa = xa + delayed * cw[CONV_WIDTH - 1 - j:CONV_WIDTH - j]

    xb = xa.astype(BF16)
    pre = [jnp.dot(xb[:, s * LRU_SUPER:(s + 1) * LRU_SUPER], wbd_ref[s], preferred_element_type=F32)
           for s in range(LRU_WIDTH // LRU_SUPER)]
    pre_a = jnp.concatenate([p[:, :LRU_SUPER] for p in pre], axis=-1)
    pre_x = jnp.concatenate([p[:, LRU_SUPER:] for p in pre], axis=-1)
    gate_a = _sigmoid(pre_a + ba_ref[...])
    gate_x = _sigmoid(pre_x + bx_ref[...])
    log_a = -LRU_C * gate_a * _softplus(-lam_ref[...])
    a = jnp.exp(log_a)
    z = 1.0 - a * a
    u = xa * gate_x * (z * lax.rsqrt(jnp.maximum(z, 1e-30)))

    a = a.reshape(groups, SUBLANES, LRU_WIDTH)
    u = u.reshape(groups, SUBLANES, LRU_WIDTH)
    d = 1
    while d < SUBLANES:
        keep = sub >= d
        u = jnp.where(keep, a * pltpu.roll(u, d, axis=1) + u, u)
        a = jnp.where(keep, a * pltpu.roll(a, d, axis=1), a)
        d *= 2
    carry = h_ref[0:1, :]
    hs = []
    for g in range(groups):
        hs.append(a[g] * carry + u[g])
        carry = hs[-1][SUBLANES - 1:SUBLANES, :]
    h = jnp.concatenate(hs, axis=0)
    h_ref[...] = jnp.broadcast_to(carry, h_ref.shape)

    y = jnp.dot(hin, wp_ref[:, LRU_WIDTH:], preferred_element_type=F32)
    gelu = 0.5 * y * (1.0 + jnp.tanh(0.7978845608028654 * (y + 0.044715 * (y * y * y))))
    o_ref[0] = (h * gelu).astype(o_ref.dtype)


def _lru_branch(h, w_lru, cw, cb, wbd, ba, bx, lam, tm):
    B, S, D = h.shape
    row = lambda n: pl.BlockSpec((1, n), lambda b, i: (0, 0))
    return pl.pallas_call(
        functools.partial(_lru_kernel, tm=tm),
        grid=(B, S // tm),
        in_specs=[
            pl.BlockSpec((1, tm, D), lambda b, i: (b, i, 0)),
            pl.BlockSpec((D, 2 * LRU_WIDTH), lambda b, i: (0, 0)),
            pl.BlockSpec((CONV_WIDTH, LRU_WIDTH), lambda b, i: (0, 0)),
            row(LRU_WIDTH),
            pl.BlockSpec((LRU_WIDTH // LRU_SUPER, LRU_SUPER, 2 * LRU_SUPER), lambda b, i: (0, 0, 0)),
            row(LRU_WIDTH), row(LRU_WIDTH), row(LRU_WIDTH),
        ],
        out_specs=pl.BlockSpec((1, tm, LRU_WIDTH), lambda b, i: (b, i, 0)),
        out_shape=jax.ShapeDtypeStruct((B, S, LRU_WIDTH), BF16),
        scratch_shapes=[pltpu.VMEM((SUBLANES, LRU_WIDTH), F32), pltpu.VMEM((SUBLANES, LRU_WIDTH), F32)],
        compiler_params=_params(("arbitrary", "arbitrary")),
        name="lru_branch",
    )(h, w_lru, cw, cb.reshape(1, -1), wbd, ba.reshape(1, -1), bx.reshape(1, -1), lam.reshape(1, -1))


def _rw_prep_kernel(*refs, tm, has_mix):
    if has_mix:
        (h_ref, wp_ref, mu_ref, w0_ref, a0_ref, w2_ref, a2_ref, g2_ref, vf_ref, v0_ref, v2_ref,
         r_ref, lw_ref, k_ref, v_ref, a_ref, g_ref, prev_ref) = refs
    else:
        (h_ref, wp_ref, mu_ref, w0_ref, a0_ref, w2_ref, a2_ref, g2_ref,
         r_ref, lw_ref, k_ref, v_ref, a_ref, g_ref, prev_ref) = refs

    @pl.when(pl.program_id(1) == 0)
    def _():
        prev_ref[...] = jnp.zeros_like(prev_ref)

    h = h_ref[0]
    ncol = wp_ref.shape[-1]
    groups = tm // SUBLANES

    def project(j):
        return jnp.dot(h, wp_ref[:, j:min(j + MM_N_CHUNK, ncol)], preferred_element_type=F32)

    def token_shift(x, j):
        cols = slice(j, j + x.shape[-1])
        rot = pltpu.roll(jnp.concatenate([prev_ref[:, cols], x], axis=0).reshape(groups + 1, SUBLANES, x.shape[-1]),
                         1, axis=1)
        sub = lax.broadcasted_iota(jnp.int32, (groups, SUBLANES, x.shape[-1]), 1)
        xs = jnp.where(sub >= 1, rot[1:], rot[:-1]).reshape(x.shape)
        prev_ref[:, cols] = x[tm - SUBLANES:]
        return x + (xs - x) * mu_ref[:, cols]

    p_lora = project(RW_LORA_OFF)
    p_next = project(0)

    lg = token_shift(p_lora[:, :RW_COLS_PAD - RW_LORA_OFF], RW_LORA_OFF)
    lora = lg[:, :LANES]
    mix = []

    def head_mix():
        if has_mix:
            hv = p_lora[:, RW_COLS_PAD - RW_LORA_OFF:]
            mix.append(_sigmoid(v0_ref[...] + jnp.dot(hv.astype(BF16), v2_ref[...], preferred_element_type=F32)))

    def head_decay():
        w = w0_ref[...] + jnp.dot(jnp.tanh(lora).astype(BF16), w2_ref[...], preferred_element_type=F32)
        w = -_softplus(-w) - 0.5
        lw_ref[0] = -jnp.exp(w)

    def head_rate():
        a_ref[0] = _sigmoid(a0_ref[...] + jnp.dot(lora.astype(BF16), a2_ref[...], preferred_element_type=F32))

    def head_gate():
        g_ref[0] = jnp.dot(_sigmoid(lg[:, LANES:]).astype(BF16), g2_ref[...], preferred_element_type=F32)

    lora_heads = [head_mix, head_decay, head_rate, head_gate]
    for j in range(0, RW_LORA_OFF, MM_N_CHUNK):
        p_cur = p_next
        if j + MM_N_CHUNK < RW_LORA_OFF:
            p_next = project(j + MM_N_CHUNK)
        if lora_heads:
            lora_heads.pop(0)()
        out_ref = (r_ref, k_ref, v_ref)[j // RW_WIDTH]
        cols = slice(j % RW_WIDTH, j % RW_WIDTH + MM_N_CHUNK)
        val = token_shift(p_cur, j)
        if has_mix and out_ref is v_ref:
            val = val + (vf_ref[0, :, cols] - val) * mix[0][:, cols]
        out_ref[0, :, cols] = val


def _rw_prep(h, w_rw, mu, w0, a0, w2p, a2p, g2p, tm, mix=None):
    B, S, D = h.shape
    ncol = w_rw.shape[1]
    has_mix = mix is not None
    const = lambda shape: pl.BlockSpec(shape, lambda b, i: (0,) * len(shape))
    tile = lambda n: pl.BlockSpec((1, tm, n), lambda b, i: (b, i, 0))
    in_specs = [tile(D), const((D, ncol)), const((1, RW_COLS_PAD)), const((1, RW_WIDTH)), const((1, RW_WIDTH)),
                const((LANES, RW_WIDTH)), const((LANES, RW_WIDTH)), const((RW_GATE_PAD, RW_WIDTH))]
    args = [h, w_rw, mu, w0, a0, w2p, a2p, g2p]
    if has_mix:
        v_first, v0, v2p = mix
        in_specs += [tile(RW_WIDTH), const((1, RW_WIDTH)), const((RW_MV_PAD, RW_WIDTH))]
        args += [v_first, v0, v2p]
    out = jax.ShapeDtypeStruct((B, S, RW_WIDTH), F32)
    return pl.pallas_call(
        functools.partial(_rw_prep_kernel, tm=tm, has_mix=has_mix),
        grid=(B, S // tm),
        in_specs=in_specs,
        out_specs=[tile(RW_WIDTH)] * 6,
        out_shape=[out] * 6,
        scratch_shapes=[pltpu.VMEM((SUBLANES, RW_COLS_PAD), F32)],
        compiler_params=_params(("arbitrary", "arbitrary")),
        name="rw_prep",
    )(*args)


def _dot(a, b):
    return jnp.dot(a, b, preferred_element_type=F32)


def _dot_nt(a, b):
    return lax.dot_general(a, b, (((1,), (1,)), ((), ())), preferred_element_type=F32)


def _split(x):
    hi = x.astype(BF16)
    return hi, (x - hi.astype(F32)).astype(BF16)


def _seg_dot(x, seg2):
    hi, lo = _split(x)
    return _dot(jnp.concatenate([hi, lo], axis=1), seg2)


def _rw_scan_kernel(r_ref, lw_ref, k_ref, v_ref, a_ref, g_ref, kk_ref, ka_ref, rk_ref, lnw_ref, lnb_ref,
                    o_ref, m_ref, y_ref, *, ts):
    C = RW_CHUNK
    N = 2 * C

    @pl.when(pl.program_id(2) == 0)
    def _():
        m_ref[...] = jnp.zeros_like(m_ref)

    ri = lax.broadcasted_iota(jnp.int32, (N, N), 0)
    ci = lax.broadcasted_iota(jnp.int32, (N, N), 1)
    same = (ri >= C) == (ci >= C)
    strict = same & (ri > ci)
    incl = same & (ri >= ci)
    eye = (ri == ci).astype(F32)
    seg = jnp.concatenate([same.astype(BF16)] * 2, axis=0)
    tri2 = jnp.concatenate([incl.astype(BF16)] * 2, axis=1)
    lane_lo = lax.broadcasted_iota(jnp.int32, (C, N), 1) < RW_HEAD

    def stack_masked(x):
        return jnp.concatenate([jnp.where(lane_lo, x, 0.0), jnp.where(lane_lo, 0.0, x)], axis=0)

    def stack(x):
        return jnp.concatenate([x, x], axis=0)

    kk_p = kk_ref[...]
    ka_p = ka_ref[...]
    rows = lambda x, c: x[c * C:(c + 1) * C]

    def phase1(first):
        chunks = range(RW_GROUP)
        gs = slice(first * C, (first + RW_GROUP) * C)
        r = r_ref[0, gs, :]
        lw = lw_ref[0, gs, :]
        v = v_ref[0, gs, :]
        a = a_ref[0, gs, :]
        kkr = k_ref[0, gs, :] * kk_p
        kk = kkr * lax.rsqrt(jnp.maximum(_seg_dot(kkr * kkr, seg), 1e-24))
        k = k_ref[0, gs, :] * (1.0 + (a - 1.0) * ka_p)
        b_s = kk * a

        lw_hi, lw_lo = _split(lw)
        wide = lambda x: jnp.concatenate([stack(rows(x, c)) for c in chunks], axis=1)
        cum_w = _dot(tri2, jnp.concatenate([wide(lw_hi), wide(lw_lo)], axis=0))
        cum = jnp.concatenate([cum_w[:C, c * N:(c + 1) * N] for c in chunks], axis=0)
        last = [cum_w[C - 1:C, c * N:(c + 1) * N] for c in chunks]
        last_b = jnp.concatenate([jnp.broadcast_to(l, (C, N)) for l in last], axis=0)
        e_inv = jnp.exp(-cum)
        e_out = jnp.exp(last_b - cum)
        a_t = -kk * jnp.exp(cum - lw)
        r_t = r * jnp.exp(cum)
        b_in = b_s * e_inv
        k_in = k * e_inv
        b_out = b_s * e_out
        k_out = k * e_out
        yield None

        a_st = [stack_masked(rows(a_t, c)).astype(BF16) for c in chunks]
        r_st = [stack_masked(rows(r_t, c)).astype(BF16) for c in chunks]
        v_st = [stack_masked(rows(v, c)).astype(BF16) for c in chunks]
        rhs = [jnp.concatenate([stack(rows(b_in, c)), stack(rows(k_in, c))], axis=0).astype(BF16) for c in chunks]
        gram = [_dot_nt(jnp.concatenate([a_st[c], r_st[c]], axis=0), rhs[c]) for c in chunks]
        ab = [jnp.where(strict, g[:N, :N], 0.0) for g in gram]
        ak = [jnp.where(strict, g[:N, N:], 0.0).astype(BF16) for g in gram]
        rb = [jnp.where(incl, g[N:, :N], 0.0).astype(BF16) for g in gram]
        rk = [jnp.where(incl, g[N:, N:], 0.0).astype(BF16) for g in gram]
        yield None

        s_inv = [eye + x for x in ab]
        q = [x.astype(BF16) for x in ab]
        q = [_dot(x, x) for x in q]
        yield None
        for _ in range(4):
            q_b = [x.astype(BF16) for x in q]
            prod = [_dot(q_b[c], jnp.concatenate([q_b[c], s_inv[c].astype(BF16)], axis=1)) for c in chunks]
            q = [p[:, :N] for p in prod]
            s_inv = [s + p[:, N:] for s, p in zip(s_inv, prod)]
            yield None
        t_b = [(s + _dot(x.astype(BF16), s.astype(BF16))).astype(BF16) for s, x in zip(s_inv, q)]
        yield None

        akv = [_dot(ak[c], v_st[c]).astype(BF16) for c in chunks]
        yield None
        uw = [_dot(t_b[c], jnp.concatenate([akv[c], a_st[c]], axis=1)).astype(BF16) for c in chunks]
        bh_t = [stack_masked(rows(b_out, c)).T.astype(BF16) for c in chunks]
        kh_t = [stack_masked(rows(k_out, c)).T.astype(BF16) for c in chunks]
        yield None
        zero = jnp.zeros((N, N), BF16)
        left = [jnp.concatenate([jnp.concatenate([bh_t[c], kh_t[c]], axis=1),
                                 jnp.concatenate([rb[c], rk[c]], axis=1)], axis=0) for c in chunks]
        right = [jnp.concatenate([jnp.concatenate([uw[c][:, N:], uw[c][:, :N]], axis=1),
                                  jnp.concatenate([zero, v_st[c]], axis=1)], axis=0) for c in chunks]
        both = [_dot(left[c], right[c]) for c in chunks]
        lin = [jnp.concatenate([both[c][:N, :N], stack_masked(rows(r_t, c)) + both[c][N:, :N]],
                               axis=0).astype(BF16) for c in chunks]
        decay_col = [jnp.broadcast_to(jnp.exp(l), (N, N)).T for l in last]
        yield [(first + c, lin[c], both[c][:N, N:], both[c][N:, N:], decay_col[c]) for c in chunks]

    state = [m_ref[...]]

    def phase2_step(item):
        c, lin, m_add, y_add, decay_col = item
        m = state[0]
        x = _dot(lin, m.astype(BF16))
        y_st = x[N:] + y_add
        state[0] = decay_col * m + x[:N] + m_add
        y_ref[c * C:(c + 1) * C, :] = y_st[:C] + y_st[C:]

    pending = []
    for first in range(0, ts // C, RW_GROUP):
        maps = None
        for item in phase1(first):
            if item is not None:
                maps = item
            elif pending:
                phase2_step(pending.pop(0))
        while pending:
            phase2_step(pending.pop(0))
        pending = list(maps)
    while pending:
        phase2_step(pending.pop(0))
    m_ref[...] = state[0]

    y = y_ref[...]
    a = a_ref[0]
    k = k_ref[0] * (1.0 + (a - 1.0) * ka_p)
    mean = _seg_dot(y, seg) * (1.0 / RW_HEAD)
    yc = y - mean
    var = _seg_dot(yc * yc, seg) * (1.0 / RW_HEAD)
    yn = yc * lax.rsqrt(var + RW_LN_EPS) * lnw_ref[...] + lnb_ref[...]
    bonus = _seg_dot(r_ref[0] * k * rk_ref[...], seg) * v_ref[0]
    o_ref[0] = ((yn + bonus) * g_ref[0]).astype(o_ref.dtype)


def _rw_scan(r, lw, k, v, a, g, kk_p, ka_p, rk_p, lnw, lnb, ts):
    B, S, W = r.shape
    tile = pl.BlockSpec((1, ts, LANES), lambda b, h, i: (b, i, h))
    vec = pl.BlockSpec((1, LANES), lambda b, h, i: (0, h))
    return pl.pallas_call(
        functools.partial(_rw_scan_kernel, ts=ts),
        grid=(B, W // LANES, S // ts),
        in_specs=[tile] * 6 + [vec] * 5,
        out_specs=tile,
        out_shape=jax.ShapeDtypeStruct((B, S, W), BF16),
        scratch_shapes=[pltpu.VMEM((LANES, LANES), F32), pltpu.VMEM((ts, LANES), F32)],
        compiler_params=_params(("arbitrary", "arbitrary", "arbitrary")),
        name="rw_scan",
    )(r, lw, k, v, a, g, kk_p, ka_p, rk_p, lnw, lnb)


def _attn_kernel(q_ref, k_ref, v_ref, qg_ref, kg_ref, sink_ref, o_ref, kpad_ref, vpad_ref):
    W = WINDOW
    n = pl.program_id(1)

    band_lo = lax.broadcasted_iota(jnp.int32, (2 * W, LANES), 1) < ATT_HEAD
    ones_even = jnp.where(band_lo, 1.0, 0.0).astype(BF16)
    ones_odd = jnp.where(band_lo, 0.0, 1.0).astype(BF16)

    @pl.when(n == 0)
    def _():
        kpad_ref[...] = jnp.zeros_like(kpad_ref)
        for t in range(2 * ATT_KV_HEADS):
            vpad_ref[t, :, 0:LANES] = jnp.zeros((2 * W, LANES), BF16)
            vpad_ref[t, :, LANES:2 * LANES] = ones_odd if t % 2 else ones_even

    kpad_ref[:, 0:W, :] = kpad_ref[:, W:2 * W, :]
    vpad_ref[:, 0:W, 0:LANES] = vpad_ref[:, W:2 * W, 0:LANES]

    ri = lax.broadcasted_iota(jnp.int32, (LANES, LANES), 0)
    ci = lax.broadcasted_iota(jnp.int32, (LANES, LANES), 1)
    seg = jnp.concatenate([((ri >= ATT_HEAD) == (ci >= ATT_HEAD)).astype(BF16)] * 2, axis=0)
    lo = lax.broadcasted_iota(jnp.int32, (W, LANES), 1) < ATT_HEAD

    def head_rms(t, gain):
        ms = _seg_dot(t * t, seg) * (1.0 / ATT_HEAD)
        return t * lax.rsqrt(ms + QK_EPS) * gain

    kg = kg_ref[...]
    for c in range(KV_WIDTH // LANES):
        cs = slice(c * LANES, (c + 1) * LANES)
        kn = head_rms(k_ref[0, :, cs], kg)
        vv = v_ref[0, :, cs]
        for t, src in ((kpad_ref, kn), (vpad_ref, vv)):
            left = jnp.where(lo, src, 0.0)
            right = jnp.where(lo, 0.0, src)
            t[4 * c + 0, W:2 * W, 0:LANES] = left.astype(BF16)
            t[4 * c + 1, W:2 * W, 0:LANES] = pltpu.roll(left, ATT_HEAD, axis=1).astype(BF16)
            t[4 * c + 2, W:2 * W, 0:LANES] = pltpu.roll(right, ATT_HEAD, axis=1).astype(BF16)
            t[4 * c + 3, W:2 * W, 0:LANES] = right.astype(BF16)

    qi = lax.broadcasted_iota(jnp.int32, (W, 2 * W), 0) + W
    kj = lax.broadcasted_iota(jnp.int32, (W, 2 * W), 1)
    mask = (kj <= qi) & (qi - kj < W) & ((n * W - W + kj) >= 0)
    mask_add = jnp.where(mask, 0.0, NEG_INF)

    qg = qg_ref[...] * (ATT_HEAD ** -0.5)
    sink = [sink_ref[hd:hd + 1, 0:1] for hd in range(ATT_HEADS)]
    slot = [2 * (hd // ATT_GROUP) + hd % 2 for hd in range(ATT_HEADS)]

    def scores(st, heads):
        pairs = sorted({hd // 2 for hd in heads})
        q = {p: head_rms(q_ref[0, :, p * LANES:(p + 1) * LANES], qg).astype(BF16) for p in pairs}
        st["s"] = {hd: _dot_nt(q[hd // 2], kpad_ref[slot[hd]]) + mask_add for hd in heads}

    def row_max(st, heads):
        st["m"] = {hd: jnp.broadcast_to(jnp.maximum(jnp.max(st["s"][hd], axis=-1, keepdims=True), sink[hd]),
                                        (W, LANES)) for hd in heads}

    def exponent(st, heads):
        st["e"] = {hd: jnp.exp(st["s"][hd] - jnp.concatenate([st["m"][hd]] * 2, axis=-1)).astype(BF16)
                   for hd in heads}
        st["sink_add"] = {hd: jnp.exp(sink[hd] - st["m"][hd]) for hd in heads}

    def weighted(st, heads):
        st["pv"] = {hd: _dot(st["e"][hd], vpad_ref[slot[hd]]) for hd in heads}

    def finish(st, heads):
        for p in sorted({hd // 2 for hd in heads}):
            both = st["pv"][2 * p] + st["pv"][2 * p + 1]
            den = both[:, LANES:] + jnp.where(lo, st["sink_add"][2 * p], st["sink_add"][2 * p + 1])
            o_ref[0, :, p * LANES:(p + 1) * LANES] = (both[:, :LANES] / den).astype(o_ref.dtype)

    stages = (scores, row_max, exponent, weighted, finish)
    groups = [list(range(j * ATT_GROUP, (j + 1) * ATT_GROUP)) for j in range(ATT_KV_HEADS)]
    states = [{} for _ in groups]
    for t in range(len(groups) + len(stages) - 1):
        for gi, heads in enumerate(groups):
            if 0 <= t - gi < len(stages):
                stages[t - gi](states[gi], heads)


def _attention(p_att, q_gain, k_gain, sinks_b):
    B, S, _ = p_att.shape
    nq = ATT_WIDTH // KV_WIDTH
    pair = lambda g: jnp.concatenate([g, g]).reshape(1, LANES)
    return pl.pallas_call(
        _attn_kernel,
        grid=(B, S // WINDOW),
        in_specs=[
            pl.BlockSpec((1, WINDOW, ATT_WIDTH), lambda b, n: (b, n, 0)),
            pl.BlockSpec((1, WINDOW, KV_WIDTH), lambda b, n: (b, n, nq)),
            pl.BlockSpec((1, WINDOW, KV_WIDTH), lambda b, n: (b, n, nq + 1)),
            pl.BlockSpec((1, LANES), lambda b, n: (0, 0)),
            pl.BlockSpec((1, LANES), lambda b, n: (0, 0)),
            pl.BlockSpec((ATT_HEADS, LANES), lambda b, n: (0, 0)),
        ],
        out_specs=pl.BlockSpec((1, WINDOW, ATT_WIDTH), lambda b, n: (b, n, 0)),
        out_shape=jax.ShapeDtypeStruct((B, S, ATT_WIDTH), BF16),
        scratch_shapes=[pltpu.VMEM((2 * ATT_KV_HEADS, 2 * WINDOW, LANES), BF16),
                        pltpu.VMEM((2 * ATT_KV_HEADS, 2 * WINDOW, 2 * LANES), BF16)],
        compiler_params=_params(("arbitrary", "arbitrary")),
        name="swa_attention",
    )(p_att, p_att, p_att, pair(q_gain), pair(k_gain), sinks_b)


def _merge_kernel(ya_ref, yb_ref, yc_ref, gt_ref, x_ref, g1_ref, wa_ref, wb_ref, wc_ref, wo_ref, o_ref):
    gt = gt_ref[0]
    o_a = jnp.dot(ya_ref[0], wa_ref[...], preferred_element_type=F32)
    mixed = _sigmoid(gt[:, 0:D_MODEL]) * o_a
    o_b = jnp.dot(yb_ref[0], wb_ref[...], preferred_element_type=F32)
    mixed = mixed + _sigmoid(gt[:, D_MODEL:2 * D_MODEL]) * o_b
    o_c = jnp.dot(yc_ref[0], wc_ref[...], preferred_element_type=F32)
    mixed = mixed + _sigmoid(gt[:, 2 * D_MODEL:3 * D_MODEL]) * o_c
    out = jnp.dot(mixed.astype(BF16), wo_ref[...], preferred_element_type=F32)
    o_ref[0] = x_ref[0] + g1_ref[0] * out


def _merge(ya, yb, yc, gates, x, g1, wa, wb, wc, wo, tm):
    B, S, D = x.shape
    tile = lambda n: pl.BlockSpec((1, tm, n), lambda b, i: (b, i, 0))
    const = lambda a: pl.BlockSpec(a.shape, lambda b, i: (0, 0))
    return pl.pallas_call(
        _merge_kernel,
        grid=(B, S // tm),
        in_specs=[tile(LRU_WIDTH), tile(RW_WIDTH), tile(ATT_WIDTH), tile(3 * D), tile(D),
                  pl.BlockSpec((1, 1, D), lambda b, i: (b, 0, 0)),
                  const(wa), const(wb), const(wc), const(wo)],
        out_specs=tile(D),
        out_shape=jax.ShapeDtypeStruct((B, S, D), F32),
        compiler_params=_params(("arbitrary", "arbitrary")),
        name="merge",
    )(ya, yb, yc, gates, x, g1, wa, wb, wc, wo)


FFN_CHUNKS = ((0, 1024), (1024, 1024), (2048, 768))


def _ffn_kernel(x_ref, sc_ref, sh_ref, g2_ref, wi_ref, wo_ref, o_ref):
    x = x_ref[0]
    ms = jnp.mean(x * x, axis=-1, keepdims=True)
    h = (x * lax.rsqrt(ms + RMS_EPS) * (1.0 + sc_ref[0]) + sh_ref[0]).astype(BF16)
    acc = jnp.zeros(x.shape, F32)
    for off, width in FFN_CHUNKS:
        gate = jnp.dot(h, wi_ref[:, off:off + width], preferred_element_type=F32)
        up = jnp.dot(h, wi_ref[:, D_FF + off:D_FF + off + width], preferred_element_type=F32)
        act = (gate * _sigmoid(gate) * up).astype(BF16)
        acc = acc + jnp.dot(act, wo_ref[off:off + width, :], preferred_element_type=F32)
    o_ref[0] = x + g2_ref[0] * acc


def _ffn(x, sc, sh, g2, wi, wo, tm):
    B, S, D = x.shape
    tile = pl.BlockSpec((1, tm, D), lambda b, i: (b, i, 0))
    vec = pl.BlockSpec((1, 1, D), lambda b, i: (b, 0, 0))
    return pl.pallas_call(
        _ffn_kernel,
        grid=(B, S // tm),
        in_specs=[tile, vec, vec, vec,
                  pl.BlockSpec(wi.shape, lambda b, i: (0, 0)),
                  pl.BlockSpec(wo.shape, lambda b, i: (0, 0))],
        out_specs=tile,
        out_shape=jax.ShapeDtypeStruct((B, S, D), F32),
        compiler_params=_params(("arbitrary", "arbitrary")),
        name="ffn",
    )(x, sc, sh, g2, wi, wo)


def _pad_rows(w, rows):
    return jnp.pad(w, ((0, rows - w.shape[0]), (0, 0)))


def _pad_cols(w, cols):
    return jnp.pad(w, ((0, 0), (0, cols - w.shape[1])))


def _block_diag_gates(wa, wx):
    per = LRU_SUPER // LRU_BLOCK
    n_super = LRU_WIDTH // LRU_SUPER
    on_diag = jnp.eye(per, dtype=bool)[None, :, None, :, None]

    def block_diag(w):
        w = w.reshape(n_super, per, LRU_BLOCK, 1, LRU_BLOCK)
        return jnp.where(on_diag, w, 0.0).reshape(n_super, LRU_SUPER, LRU_SUPER)

    return jnp.concatenate([block_diag(wa), block_diag(wx)], axis=2).astype(BF16)


def _tile_rows(S, want):
    t = min(want, S)
    while S % t:
        t //= 2
    return t


def kernel(x, c, w_ada, b_ada, w_in, conv_w, conv_b, lru_wa, lru_ba, lru_wx, lru_bx, lru_lambda, w_lru_o, rw_mu, rw_w0, rw_w2, rw_a0, rw_a2, rw_g2, rw_kk, rw_ka, rw_rk, rw_ln_w, rw_ln_b, rw_v0, rw_v1, rw_v2, w_rw_o, q_gain, k_gain, sinks, w_att_o, w_out, w_ffn_in, w_ffn_out):
    B, S, D = x.shape
    depth = w_in.shape[0]
    assert D == D_MODEL and S % WINDOW == 0
    tm_mm = _tile_rows(S, 512)
    tm_seq = _tile_rows(S, 256)
    ts_scan = _tile_rows(S, RW_GROUPS_PER_STEP * RW_GROUP * RW_CHUNK)

    mod = _ada(c, w_ada, b_ada).reshape(depth, B, 6, 1, D)

    o_lru = 2 * LRU_WIDTH
    o_rw = o_lru + 3 * RW_WIDTH + RW_DECAY_LORA + RW_AAA_LORA + RW_GATE_LORA
    o_att = o_rw + ATT_WIDTH + 2 * KV_WIDTH

    v_first = None
    for i in range(depth):
        sh1, sc1, g1, sh2, sc2, g2 = (mod[i, :, j] for j in range(6))
        wi = w_in[i]
        w_lru = wi[:, :o_lru].astype(BF16)
        w_rw_cols = [_pad_cols(wi[:, o_lru:o_rw], RW_COLS_PAD)]
        if i > 0:
            w_rw_cols.append(_pad_cols(rw_v1[i - 1], RW_MV_PAD))
        w_rw = jnp.concatenate(w_rw_cols, axis=1).astype(BF16)
        w_att = wi[:, o_rw:o_att].astype(BF16)
        w_gate = wi[:, o_att:].astype(BF16)

        h = _norm_mod(x, sc1, sh1, tm_mm)
        p_att = _mm(h, w_att, tm_mm, F32, "proj_att")
        p_gate = _mm(h, w_gate, tm_mm, F32, "proj_gate")

        ya = _lru_branch(h, w_lru, conv_w[i], conv_b[i], _block_diag_gates(lru_wa[i], lru_wx[i]),
                         lru_ba[i], lru_bx[i], lru_lambda[i], tm_seq)

        mu = _pad_cols(rw_mu[i].reshape(1, -1), RW_COLS_PAD)
        w2p = _pad_rows(rw_w2[i], LANES).astype(BF16)
        a2p = jnp.concatenate([jnp.zeros_like(rw_a2[i]), rw_a2[i]], axis=0).astype(BF16)
        g2p = _pad_rows(rw_g2[i], RW_GATE_PAD).astype(BF16)
        mix = None
        if i > 0:
            mix = (v_first, rw_v0[i - 1].reshape(1, -1), _pad_rows(rw_v2[i - 1], RW_MV_PAD).astype(BF16))
        r, lw, k, v, a, g = _rw_prep(h, w_rw, mu, rw_w0[i].reshape(1, -1), rw_a0[i].reshape(1, -1),
                                     w2p, a2p, g2p, tm_seq, mix)
        if i == 0:
            v_first = v
        yb = _rw_scan(r, lw, k, v, a, g, rw_kk[i].reshape(1, -1), rw_ka[i].reshape(1, -1),
                      rw_rk[i].reshape(1, -1), rw_ln_w[i].reshape(1, -1), rw_ln_b[i].reshape(1, -1), ts_scan)

        sinks_b = jnp.broadcast_to(sinks[i].reshape(-1, 1), (ATT_HEADS, LANES))
        yc = _attention(p_att, q_gain[i], k_gain[i], sinks_b)

        x = _merge(ya, yb, yc, p_gate, x, g1, w_lru_o[i].astype(BF16), w_rw_o[i].astype(BF16),
                   w_att_o[i].astype(BF16), w_out[i].astype(BF16), tm_mm)
        x = _ffn(x, sc2, sh2, g2, w_ffn_in[i].astype(BF16), w_ffn_out[i].astype(BF16), tm_mm)
    return x
```

```python
import functools

import jax
import jax.numpy as jnp
from jax import lax
from jax.experimental import pallas as pl
from jax.experimental.pallas import tpu as pltpu

D_MODEL = 1024
LRU_WIDTH = 1280
LRU_BLOCKS = 16
LRU_BLOCK = 80
LRU_SUPER = 640
CONV_WIDTH = 4
LRU_C = 8.0
RW_HEAD = 64
RW_WIDTH = 1024
RW_DECAY_LORA = 64
RW_AAA_LORA = 64
RW_MV_LORA = 32
RW_GATE_LORA = 160
RW_LN_EPS = 64e-5
ATT_HEAD = 64
ATT_HEADS = 16
ATT_KV_HEADS = 4
ATT_GROUP = 4
ATT_WIDTH = 1024
KV_WIDTH = 256
WINDOW = 128
D_FF = 2816
RMS_EPS = 1e-6
QK_EPS = 1e-6
NEG_INF = -1e30

LANES = 128
SUBLANES = 8
VMEM_LIMIT_BYTES = 56 * 1024 * 1024

RW_LORA_OFF = 3 * RW_WIDTH
RW_GATE_OFF = RW_LORA_OFF + LANES
RW_GATE_PAD = 2 * LANES
RW_COLS_PAD = RW_GATE_OFF + RW_GATE_PAD
RW_MV_PAD = LANES

MM_N_CHUNK = 512
RW_CHUNK = 64
RW_GROUP = 8
RW_GROUPS_PER_STEP = 8
ATT_BLOCKS_PER_STEP = 4

F32 = jnp.float32
BF16 = jnp.bfloat16
HIGHEST = lax.Precision.HIGHEST


def _sigmoid(x):
    return 0.5 * jnp.tanh(0.5 * x) + 0.5


def _softplus(z):
    return jnp.maximum(z, 0.0) + jnp.log(1.0 + jnp.exp(-jnp.abs(z)))


def _params(sem):
    return pltpu.CompilerParams(dimension_semantics=sem, vmem_limit_bytes=VMEM_LIMIT_BYTES)


def _ada_kernel(c_ref, w_ref, b_ref, o_ref):
    c = c_ref[...]
    act = (c * _sigmoid(c)).astype(BF16)
    o_ref[0] = jnp.dot(act, w_ref[0].astype(BF16), preferred_element_type=F32) + b_ref[0]


def _ada(c, w_ada, b_ada):
    L, D, N = w_ada.shape
    B = c.shape[0]
    tn = D_MODEL
    return pl.pallas_call(
        _ada_kernel,
        grid=(L, N // tn),
        in_specs=[
            pl.BlockSpec((B, D), lambda l, j: (0, 0)),
            pl.BlockSpec((1, D, tn), lambda l, j: (l, 0, j)),
            pl.BlockSpec((1, 1, tn), lambda l, j: (l, 0, j)),
        ],
        out_specs=pl.BlockSpec((1, B, tn), lambda l, j: (l, 0, j)),
        out_shape=jax.ShapeDtypeStruct((L, B, N), F32),
        compiler_params=_params(("arbitrary", "arbitrary")),
        name="ada_mod",
    )(c, w_ada, b_ada.reshape(L, 1, N))


def _norm_mod_kernel(x_ref, sc_ref, sh_ref, o_ref):
    x = x_ref[0]
    ms = jnp.mean(x * x, axis=-1, keepdims=True)
    h = x * lax.rsqrt(ms + RMS_EPS) * (1.0 + sc_ref[0]) + sh_ref[0]
    o_ref[0] = h.astype(o_ref.dtype)


def _norm_mod(x, sc, sh, tm):
    B, S, D = x.shape
    vec = pl.BlockSpec((1, 1, D), lambda b, i: (b, 0, 0))
    return pl.pallas_call(
        _norm_mod_kernel,
        grid=(B, S // tm),
        in_specs=[pl.BlockSpec((1, tm, D), lambda b, i: (b, i, 0)), vec, vec],
        out_specs=pl.BlockSpec((1, tm, D), lambda b, i: (b, i, 0)),
        out_shape=jax.ShapeDtypeStruct((B, S, D), BF16),
        compiler_params=_params(("arbitrary", "arbitrary")),
        name="norm_mod",
    )(x, sc, sh)


def _mm_kernel(x_ref, w_ref, o_ref, *, n_chunk):
    x = x_ref[0]
    n = o_ref.shape[-1]
    for j in range(0, n, n_chunk):
        w = min(n_chunk, n - j)
        o_ref[0, :, j:j + w] = jnp.dot(x, w_ref[:, j:j + w], preferred_element_type=F32).astype(o_ref.dtype)


def _mm(x, w, tm, out_dtype, name):
    B, S, K = x.shape
    N = w.shape[1]
    return pl.pallas_call(
        functools.partial(_mm_kernel, n_chunk=MM_N_CHUNK),
        grid=(B, S // tm),
        in_specs=[
            pl.BlockSpec((1, tm, K), lambda b, i: (b, i, 0)),
            pl.BlockSpec((K, N), lambda b, i: (0, 0)),
        ],
        out_specs=pl.BlockSpec((1, tm, N), lambda b, i: (b, i, 0)),
        out_shape=jax.ShapeDtypeStruct((B, S, N), out_dtype),
        compiler_params=_params(("arbitrary", "arbitrary")),
        name=name,
    )(x, w)


def _lru_kernel(hin_ref, wp_ref, cw_ref, cb_ref, wbd_ref, ba_ref, bx_ref, lam_ref, o_ref, xbuf_ref, h_ref, *, tm):
    @pl.when(pl.program_id(1) == 0)
    def _():
        xbuf_ref[...] = jnp.zeros_like(xbuf_ref)
        h_ref[...] = jnp.zeros_like(h_ref)

    hin = hin_ref[0]
    x = jnp.dot(hin, wp_ref[:, :LRU_WIDTH], preferred_element_type=F32)
    groups = tm // SUBLANES
    xg = jnp.concatenate([xbuf_ref[...], x], axis=0).reshape(groups + 1, SUBLANES, LRU_WIDTH)
    xbuf_ref[...] = x[tm - SUBLANES:]
    sub = lax.broadcasted_iota(jnp.int32, (groups, SUBLANES, LRU_WIDTH), 1)
    cw = cw_ref[...]
    xa = cb_ref[...] + x * cw[CONV_WIDTH - 1:CONV_WIDTH]
    for j in range(1, CONV_WIDTH):
        rot = pltpu.roll(xg, j, axis=1)
        delayed = jnp.where(sub >= j, rot[1:], rot[:-1]).reshape(tm, LRU_WIDTH)
        xa = xa + delayed * cw[CONV_WIDTH - 1 - j:CONV_WIDTH - j]

    xb = xa.astype(BF16)
    pre = [jnp.dot(xb[:, s * LRU_SUPER:(s + 1) * LRU_SUPER], wbd_ref[s], preferred_element_type=F32)
           for s in range(LRU_WIDTH // LRU_SUPER)]
    pre_a = jnp.concatenate([p[:, :LRU_SUPER] for p in pre], axis=-1)
    pre_x = jnp.concatenate([p[:, LRU_SUPER:] for p in pre], axis=-1)
    gate_a = _sigmoid(pre_a + ba_ref[...])
    gate_x = _sigmoid(pre_x + bx_ref[...])
    log_a = -LRU_C * gate_a * _softplus(-lam_ref[...])
    a = jnp.exp(log_a)
    z = 1.0 - a * a
    u = xa * gate_x * (z * lax.rsqrt(jnp.maximum(z, 1e-30)))

    a = a.reshape(groups, SUBLANES, LRU_WIDTH)
    u = u.reshape(groups, SUBLANES, LRU_WIDTH)
    d = 1
    while d < SUBLANES:
        keep = sub >= d
        u = jnp.where(keep, a * pltpu.roll(u, d, axis=1) + u, u)
        a = jnp.where(keep, a * pltpu.roll(a, d, axis=1), a)
        d *= 2
    carry = h_ref[0:1, :]
    hs = []
    for g in range(groups):
        hs.append(a[g] * carry + u[g])
        carry = hs[-1][SUBLANES - 1:SUBLANES, :]
    h = jnp.concatenate(hs, axis=0)
    h_ref[...] = jnp.broadcast_to(carry, h_ref.shape)

    y = jnp.dot(hin, wp_ref[:, LRU_WIDTH:], preferred_element_type=F32)
    gelu = 0.5 * y * (1.0 + jnp.tanh(0.7978845608028654 * (y + 0.044715 * (y * y * y))))
    o_ref[0] = (h * gelu).astype(o_ref.dtype)


def _lru_branch(h, w_lru, cw, cb, wbd, ba, bx, lam, tm):
    B, S, D = h.shape
    row = lambda n: pl.BlockSpec((1, n), lambda b, i: (0, 0))
    return pl.pallas_call(
        functools.partial(_lru_kernel, tm=tm),
        grid=(B, S // tm),
        in_specs=[
            pl.BlockSpec((1, tm, D), lambda b, i: (b, i, 0)),
            pl.BlockSpec((D, 2 * LRU_WIDTH), lambda b, i: (0, 0)),
            pl.BlockSpec((CONV_WIDTH, LRU_WIDTH), lambda b, i: (0, 0)),
            row(LRU_WIDTH),
            pl.BlockSpec((LRU_WIDTH // LRU_SUPER, LRU_SUPER, 2 * LRU_SUPER), lambda b, i: (0, 0, 0)),
            row(LRU_WIDTH), row(LRU_WIDTH), row(LRU_WIDTH),
        ],
        out_specs=pl.BlockSpec((1, tm, LRU_WIDTH), lambda b, i: (b, i, 0)),
        out_shape=jax.ShapeDtypeStruct((B, S, LRU_WIDTH), BF16),
        scratch_shapes=[pltpu.VMEM((SUBLANES, LRU_WIDTH), F32), pltpu.VMEM((SUBLANES, LRU_WIDTH), F32)],
        compiler_params=_params(("arbitrary", "arbitrary")),
        name="lru_branch",
    )(h, w_lru, cw, cb.reshape(1, -1), wbd, ba.reshape(1, -1), bx.reshape(1, -1), lam.reshape(1, -1))


def _rw_prep_kernel(*refs, tm, has_mix):
    if has_mix:
        (h_ref, wp_ref, mu_ref, w0_ref, a0_ref, w2_ref, a2_ref, g2_ref, vf_ref, v0_ref, v2_ref,
         r_ref, lw_ref, k_ref, v_ref, a_ref, g_ref, prev_ref) = refs
    else:
        (h_ref, wp_ref, mu_ref, w0_ref, a0_ref, w2_ref, a2_ref, g2_ref,
         r_ref, lw_ref, k_ref, v_ref, a_ref, g_ref, prev_ref) = refs

    @pl.when(pl.program_id(1) == 0)
    def _():
        prev_ref[...] = jnp.zeros_like(prev_ref)

    h = h_ref[0]
    ncol = wp_ref.shape[-1]
    groups = tm // SUBLANES

    def project(j):
        return jnp.dot(h, wp_ref[:, j:min(j + MM_N_CHUNK, ncol)], preferred_element_type=F32)

    def token_shift(x, j):
        cols = slice(j, j + x.shape[-1])
        rot = pltpu.roll(jnp.concatenate([prev_ref[:, cols], x], axis=0).reshape(groups + 1, SUBLANES, x.shape[-1]),
                         1, axis=1)
        sub = lax.broadcasted_iota(jnp.int32, (groups, SUBLANES, x.shape[-1]), 1)
        xs = jnp.where(sub >= 1, rot[1:], rot[:-1]).reshape(x.shape)
        prev_ref[:, cols] = x[tm - SUBLANES:]
        return x + (xs - x) * mu_ref[:, cols]

    p_lora = project(RW_LORA_OFF)
    p_next = project(0)

    lg = token_shift(p_lora[:, :RW_COLS_PAD - RW_LORA_OFF], RW_LORA_OFF)
    lora = lg[:, :LANES]
    mix = []

    def head_mix():
        if has_mix:
            hv = p_lora[:, RW_COLS_PAD - RW_LORA_OFF:]
            mix.append(_sigmoid(v0_ref[...] + jnp.dot(hv.astype(BF16), v2_ref[...], preferred_element_type=F32)))

    def head_decay():
        w = w0_ref[...] + jnp.dot(jnp.tanh(lora).astype(BF16), w2_ref[...], preferred_element_type=F32)
        w = -_softplus(-w) - 0.5
        lw_ref[0] = -jnp.exp(w)

    def head_rate():
        a_ref[0] = _sigmoid(a0_ref[...] + jnp.dot(lora.astype(BF16), a2_ref[...], preferred_element_type=F32))

    def head_gate():
        g_ref[0] = jnp.dot(_sigmoid(lg[:, LANES:]).astype(BF16), g2_ref[...], preferred_element_type=F32)

    lora_heads = [head_mix, head_decay, head_rate, head_gate]
    for j in range(0, RW_LORA_OFF, MM_N_CHUNK):
        p_cur = p_next
        if j + MM_N_CHUNK < RW_LORA_OFF:
            p_next = project(j + MM_N_CHUNK)
        if lora_heads:
            lora_heads.pop(0)()
        out_ref = (r_ref, k_ref, v_ref)[j // RW_WIDTH]
        cols = slice(j % RW_WIDTH, j % RW_WIDTH + MM_N_CHUNK)
        val = token_shift(p_cur, j)
        if has_mix and out_ref is v_ref:
            val = val + (vf_ref[0, :, cols] - val) * mix[0][:, cols]
        out_ref[0, :, cols] = val


def _rw_prep(h, w_rw, mu, w0, a0, w2p, a2p, g2p, tm, mix=None):
    B, S, D = h.shape
    ncol = w_rw.shape[1]
    has_mix = mix is not None
    const = lambda shape: pl.BlockSpec(shape, lambda b, i: (0,) * len(shape))
    tile = lambda n: pl.BlockSpec((1, tm, n), lambda b, i: (b, i, 0))
    in_specs = [tile(D), const((D, ncol)), const((1, RW_COLS_PAD)), const((1, RW_WIDTH)), const((1, RW_WIDTH)),
                const((LANES, RW_WIDTH)), const((LANES, RW_WIDTH)), const((RW_GATE_PAD, RW_WIDTH))]
    args = [h, w_rw, mu, w0, a0, w2p, a2p, g2p]
    if has_mix:
        v_first, v0, v2p = mix
        in_specs += [tile(RW_WIDTH), const((1, RW_WIDTH)), const((RW_MV_PAD, RW_WIDTH))]
        args += [v_first, v0, v2p]
    out = jax.ShapeDtypeStruct((B, S, RW_WIDTH), F32)
    return pl.pallas_call(
        functools.partial(_rw_prep_kernel, tm=tm, has_mix=has_mix),
        grid=(B, S // tm),
        in_specs=in_specs,
        out_specs=[tile(RW_WIDTH)] * 6,
        out_shape=[out] * 6,
        scratch_shapes=[pltpu.VMEM((SUBLANES, RW_COLS_PAD), F32)],
        compiler_params=_params(("arbitrary", "arbitrary")),
        name="rw_prep",
    )(*args)


def _dot(a, b):
    return jnp.dot(a, b, preferred_element_type=F32)


def _dot_nt(a, b):
    return lax.dot_general(a, b, (((1,), (1,)), ((), ())), preferred_element_type=F32)


def _split(x):
    hi = x.astype(BF16)
    return hi, (x - hi.astype(F32)).astype(BF16)


def _seg_dot(x, seg2):
    hi, lo = _split(x)
    return _dot(jnp.concatenate([hi, lo], axis=1), seg2)


def _rw_scan_kernel(r_ref, lw_ref, k_ref, v_ref, a_ref, g_ref, kk_ref, ka_ref, rk_ref, lnw_ref, lnb_ref,
                    o_ref, m_ref, y_ref, *, ts):
    C = RW_CHUNK
    N = 2 * C

    @pl.when(pl.program_id(2) == 0)
    def _():
        m_ref[...] = jnp.zeros_like(m_ref)

    ri = lax.broadcasted_iota(jnp.int32, (N, N), 0)
    ci = lax.broadcasted_iota(jnp.int32, (N, N), 1)
    same = (ri >= C) == (ci >= C)
    strict = same & (ri > ci)
    incl = same & (ri >= ci)
    eye = (ri == ci).astype(F32)
    seg = jnp.concatenate([same.astype(BF16)] * 2, axis=0)
    tri2 = jnp.concatenate([incl.astype(BF16)] * 2, axis=1)
    lane_lo = lax.broadcasted_iota(jnp.int32, (C, N), 1) < RW_HEAD

    def stack_masked(x):
        return jnp.concatenate([jnp.where(lane_lo, x, 0.0), jnp.where(lane_lo, 0.0, x)], axis=0)

    def stack(x):
        return jnp.concatenate([x, x], axis=0)

    kk_p = kk_ref[...]
    ka_p = ka_ref[...]
    rows = lambda x, c: x[c * C:(c + 1) * C]

    def phase1(first):
        chunks = range(RW_GROUP)
        gs = slice(first * C, (first + RW_GROUP) * C)
        r = r_ref[0, gs, :]
        lw = lw_ref[0, gs, :]
        v = v_ref[0, gs, :]
        a = a_ref[0, gs, :]
        kkr = k_ref[0, gs, :] * kk_p
        kk = kkr * lax.rsqrt(jnp.maximum(_seg_dot(kkr * kkr, seg), 1e-24))
        k = k_ref[0, gs, :] * (1.0 + (a - 1.0) * ka_p)
        b_s = kk * a

        lw_hi, lw_lo = _split(lw)
        wide = lambda x: jnp.concatenate([stack(rows(x, c)) for c in chunks], axis=1)
        cum_w = _dot(tri2, jnp.concatenate([wide(lw_hi), wide(lw_lo)], axis=0))
        cum = jnp.concatenate([cum_w[:C, c * N:(c + 1) * N] for c in chunks], axis=0)
        last = [cum_w[C - 1:C, c * N:(c + 1) * N] for c in chunks]
        last_b = jnp.concatenate([jnp.broadcast_to(l, (C, N)) for l in last], axis=0)
        e_inv = jnp.exp(-cum)
        e_out = jnp.exp(last_b - cum)
        a_t = -kk * jnp.exp(cum - lw)
        r_t = r * jnp.exp(cum)
        b_in = b_s * e_inv
        k_in = k * e_inv
        b_out = b_s * e_out
        k_out = k * e_out
        yield None

        a_st = [stack_masked(rows(a_t, c)).astype(BF16) for c in chunks]
        r_st = [stack_masked(rows(r_t, c)).astype(BF16) for c in chunks]
        v_st = [stack_masked(rows(v, c)).astype(BF16) for c in chunks]
        rhs = [jnp.concatenate([stack(rows(b_in, c)), stack(rows(k_in, c))], axis=0).astype(BF16) for c in chunks]
        gram = [_dot_nt(jnp.concatenate([a_st[c], r_st[c]], axis=0), rhs[c]) for c in chunks]
        ab = [jnp.where(strict, g[:N, :N], 0.0) for g in gram]
        ak = [jnp.where(strict, g[:N, N:], 0.0).astype(BF16) for g in gram]
        rb = [jnp.where(incl, g[N:, :N], 0.0).astype(BF16) for g in gram]
        rk = [jnp.where(incl, g[N:, N:], 0.0).astype(BF16) for g in gram]
        yield None

        s_inv = [eye + x for x in ab]
        q = [x.astype(BF16) for x in ab]
        q = [_dot(x, x) for x in q]
        yield None
        for _ in range(4):
            q_b = [x.astype(BF16) for x in q]
            prod = [_dot(q_b[c], jnp.concatenate([q_b[c], s_inv[c].astype(BF16)], axis=1)) for c in chunks]
            q = [p[:, :N] for p in prod]
            s_inv = [s + p[:, N:] for s, p in zip(s_inv, prod)]
            yield None
        t_b = [(s + _dot(x.astype(BF16), s.astype(BF16))).astype(BF16) for s, x in zip(s_inv, q)]
        yield None

        akv = [_dot(ak[c], v_st[c]).astype(BF16) for c in chunks]
        yield None
        uw = [_dot(t_b[c], jnp.concatenate([akv[c], a_st[c]], axis=1)).astype(BF16) for c in chunks]
        bh_t = [stack_masked(rows(b_out, c)).T.astype(BF16) for c in chunks]
        kh_t = [stack_masked(rows(k_out, c)).T.astype(BF16) for c in chunks]
        yield None
        zero = jnp.zeros((N, N), BF16)
        left = [jnp.concatenate([jnp.concatenate([bh_t[c], kh_t[c]], axis=1),
                                 jnp.concatenate([rb[c], rk[c]], axis=1)], axis=0) for c in chunks]
        right = [jnp.concatenate([jnp.concatenate([uw[c][:, N:], uw[c][:, :N]], axis=1),
                                  jnp.concatenate([zero, v_st[c]], axis=1)], axis=0) for c in chunks]
        both = [_dot(left[c], right[c]) for c in chunks]
        lin = [jnp.concatenate([both[c][:N, :N], stack_masked(rows(r_t, c)) + both[c][N:, :N]],
                               axis=0).astype(BF16) for c in chunks]
        decay_col = [jnp.broadcast_to(jnp.exp(l), (N, N)).T for l in last]
        yield [(first + c, lin[c], both[c][:N, N:], both[c][N:, N:], decay_col[c]) for c in chunks]

    state = [m_ref[...]]

    def phase2_step(item):
        c, lin, m_add, y_add, decay_col = item
        m = state[0]
        x = _dot(lin, m.astype(BF16))
        y_st = x[N:] + y_add
        state[0] = decay_col * m + x[:N] + m_add
        y_ref[c * C:(c + 1) * C, :] = y_st[:C] + y_st[C:]

    pending = []
    for first in range(0, ts // C, RW_GROUP):
        maps = None
        for item in phase1(first):
            if item is not None:
                maps = item
            elif pending:
                phase2_step(pending.pop(0))
        while pending:
            phase2_step(pending.pop(0))
        pending = list(maps)
    while pending:
        phase2_step(pending.pop(0))
    m_ref[...] = state[0]

    y = y_ref[...]
    a = a_ref[0]
    k = k_ref[0] * (1.0 + (a - 1.0) * ka_p)
    mean = _seg_dot(y, seg) * (1.0 / RW_HEAD)
    yc = y - mean
    var = _seg_dot(yc * yc, seg) * (1.0 / RW_HEAD)
    yn = yc * lax.rsqrt(var + RW_LN_EPS) * lnw_ref[...] + lnb_ref[...]
    bonus = _seg_dot(r_ref[0] * k * rk_ref[...], seg) * v_ref[0]
    o_ref[0] = ((yn + bonus) * g_ref[0]).astype(o_ref.dtype)


def _rw_scan(r, lw, k, v, a, g, kk_p, ka_p, rk_p, lnw, lnb, ts):
    B, S, W = r.shape
    tile = pl.BlockSpec((1, ts, LANES), lambda b, h, i: (b, i, h))
    vec = pl.BlockSpec((1, LANES), lambda b, h, i: (0, h))
    return pl.pallas_call(
        functools.partial(_rw_scan_kernel, ts=ts),
        grid=(B, W // LANES, S // ts),
        in_specs=[tile] * 6 + [vec] * 5,
        out_specs=tile,
        out_shape=jax.ShapeDtypeStruct((B, S, W), BF16),
        scratch_shapes=[pltpu.VMEM((LANES, LANES), F32), pltpu.VMEM((ts, LANES), F32)],
        compiler_params=_params(("arbitrary", "arbitrary", "arbitrary")),
        name="rw_scan",
    )(r, lw, k, v, a, g, kk_p, ka_p, rk_p, lnw, lnb)


def _attn_kernel(q_ref, k_ref, v_ref, qg_ref, kg_ref, sink_ref, o_ref, kpad_ref, vpad_ref):
    W = WINDOW
    NB = ATT_BLOCKS_PER_STEP
    n = pl.program_id(1)

    all_lo = lax.broadcasted_iota(jnp.int32, ((NB + 1) * W, LANES), 1) < ATT_HEAD
    ones_even = jnp.where(all_lo, 1.0, 0.0).astype(BF16)
    ones_odd = jnp.where(all_lo, 0.0, 1.0).astype(BF16)

    @pl.when(n == 0)
    def _():
        kpad_ref[...] = jnp.zeros_like(kpad_ref)
        for t in range(2 * ATT_KV_HEADS):
            vpad_ref[t, :, 0:LANES] = jnp.zeros(((NB + 1) * W, LANES), BF16)
            vpad_ref[t, :, LANES:2 * LANES] = ones_odd if t % 2 else ones_even

    kpad_ref[:, 0:W, :] = kpad_ref[:, NB * W:(NB + 1) * W, :]
    vpad_ref[:, 0:W, 0:LANES] = vpad_ref[:, NB * W:(NB + 1) * W, 0:LANES]

    ri = lax.broadcasted_iota(jnp.int32, (LANES, LANES), 0)
    ci = lax.broadcasted_iota(jnp.int32, (LANES, LANES), 1)
    seg = jnp.concatenate([((ri >= ATT_HEAD) == (ci >= ATT_HEAD)).astype(BF16)] * 2, axis=0)
    lo = lax.broadcasted_iota(jnp.int32, (W, LANES), 1) < ATT_HEAD
    new_lo = lax.broadcasted_iota(jnp.int32, (NB * W, LANES), 1) < ATT_HEAD

    def head_rms(t, gain):
        ms = _seg_dot(t * t, seg) * (1.0 / ATT_HEAD)
        return t * lax.rsqrt(ms + QK_EPS) * gain

    kg = kg_ref[...]
    for c in range(KV_WIDTH // LANES):
        cs = slice(c * LANES, (c + 1) * LANES)
        kn = head_rms(k_ref[0, :, cs], kg)
        vv = v_ref[0, :, cs]
        for t, src in ((kpad_ref, kn), (vpad_ref, vv)):
            left = jnp.where(new_lo, src, 0.0)
            right = jnp.where(new_lo, 0.0, src)
            t[4 * c + 0, W:(NB + 1) * W, 0:LANES] = left.astype(BF16)
            t[4 * c + 1, W:(NB + 1) * W, 0:LANES] = pltpu.roll(left, ATT_HEAD, axis=1).astype(BF16)
            t[4 * c + 2, W:(NB + 1) * W, 0:LANES] = pltpu.roll(right, ATT_HEAD, axis=1).astype(BF16)
            t[4 * c + 3, W:(NB + 1) * W, 0:LANES] = right.astype(BF16)

    qi = lax.broadcasted_iota(jnp.int32, (W, 2 * W), 0) + W
    kj = lax.broadcasted_iota(jnp.int32, (W, 2 * W), 1)
    local = (kj <= qi) & (qi - kj < W)
    mask_add = [jnp.where(local & ((n * NB * W - W + kj) >= 0), 0.0, NEG_INF)]
    mask_add += [jnp.where(local, 0.0, NEG_INF)] * (NB - 1)

    qg = qg_ref[...] * (ATT_HEAD ** -0.5)
    sink = [sink_ref[hd:hd + 1, 0:1] for hd in range(ATT_HEADS)]
    slot = [2 * (hd // ATT_GROUP) + hd % 2 for hd in range(ATT_HEADS)]

    def scores(st, b, heads):
        pairs = sorted({hd // 2 for hd in heads})
        q = {p: head_rms(q_ref[0, b * W:(b + 1) * W, p * LANES:(p + 1) * LANES], qg).astype(BF16) for p in pairs}
        st["s"] = {hd: _dot_nt(q[hd // 2], kpad_ref[slot[hd], b * W:(b + 2) * W, :]) + mask_add[b] for hd in heads}

    def row_max(st, b, heads):
        st["m"] = {hd: jnp.broadcast_to(jnp.maximum(jnp.max(st["s"][hd], axis=-1, keepdims=True), sink[hd]),
                                        (W, LANES)) for hd in heads}

    def exponent(st, b, heads):
        st["e"] = {hd: jnp.exp(st["s"][hd] - jnp.concatenate([st["m"][hd]] * 2, axis=-1)).astype(BF16)
                   for hd in heads}
        st["sink_add"] = {hd: jnp.exp(sink[hd] - st["m"][hd]) for hd in heads}

    def weighted(st, b, heads):
        st["pv"] = {hd: _dot(st["e"][hd], vpad_ref[slot[hd], b * W:(b + 2) * W, :]) for hd in heads}

    def finish(st, b, heads):
        for p in sorted({hd // 2 for hd in heads}):
            both = st["pv"][2 * p] + st["pv"][2 * p + 1]
            den = both[:, LANES:] + jnp.where(lo, st["sink_add"][2 * p], st["sink_add"][2 * p + 1])
            o_ref[0, b * W:(b + 1) * W, p * LANES:(p + 1) * LANES] = (both[:, :LANES] / den).astype(o_ref.dtype)

    stages = (scores, row_max, exponent, weighted, finish)
    tasks = [(b, list(range(j * ATT_GROUP, (j + 1) * ATT_GROUP))) for b in range(NB) for j in range(ATT_KV_HEADS)]
    states = [{} for _ in tasks]
    for t in range(len(tasks) + len(stages) - 1):
        for ti, (b, heads) in enumerate(tasks):
            if 0 <= t - ti < len(stages):
                stages[t - ti](states[ti], b, heads)


def _attention(p_att, q_gain, k_gain, sinks_b):
    B, S, _ = p_att.shape
    nq = ATT_WIDTH // KV_WIDTH
    rows = ATT_BLOCKS_PER_STEP * WINDOW
    pair = lambda g: jnp.concatenate([g, g]).reshape(1, LANES)
    return pl.pallas_call(
        _attn_kernel,
        grid=(B, S // rows),
        in_specs=[
            pl.BlockSpec((1, rows, ATT_WIDTH), lambda b, n: (b, n, 0)),
            pl.BlockSpec((1, rows, KV_WIDTH), lambda b, n: (b, n, nq)),
            pl.BlockSpec((1, rows, KV_WIDTH), lambda b, n: (b, n, nq + 1)),
            pl.BlockSpec((1, LANES), lambda b, n: (0, 0)),
            pl.BlockSpec((1, LANES), lambda b, n: (0, 0)),
            pl.BlockSpec((ATT_HEADS, LANES), lambda b, n: (0, 0)),
        ],
        out_specs=pl.BlockSpec((1, rows, ATT_WIDTH), lambda b, n: (b, n, 0)),
        out_shape=jax.ShapeDtypeStruct((B, S, ATT_WIDTH), BF16),
        scratch_shapes=[pltpu.VMEM((2 * ATT_KV_HEADS, rows + WINDOW, LANES), BF16),
                        pltpu.VMEM((2 * ATT_KV_HEADS, rows + WINDOW, 2 * LANES), BF16)],
        compiler_params=_params(("arbitrary", "arbitrary")),
        name="swa_attention",
    )(p_att, p_att, p_att, pair(q_gain), pair(k_gain), sinks_b)


def _merge_kernel(ya_ref, yb_ref, yc_ref, gt_ref, x_ref, g1_ref, wa_ref, wb_ref, wc_ref, wo_ref, o_ref):
    gt = gt_ref[0]
    o_a = jnp.dot(ya_ref[0], wa_ref[...], preferred_element_type=F32)
    mixed = _sigmoid(gt[:, 0:D_MODEL]) * o_a
    o_b = jnp.dot(yb_ref[0], wb_ref[...], preferred_element_type=F32)
    mixed = mixed + _sigmoid(gt[:, D_MODEL:2 * D_MODEL]) * o_b
    o_c = jnp.dot(yc_ref[0], wc_ref[...], preferred_element_type=F32)
    mixed = mixed + _sigmoid(gt[:, 2 * D_MODEL:3 * D_MODEL]) * o_c
    out = jnp.dot(mixed.astype(BF16), wo_ref[...], preferred_element_type=F32)
    o_ref[0] = x_ref[0] + g1_ref[0] * out


def _merge(ya, yb, yc, gates, x, g1, wa, wb, wc, wo, tm):
    B, S, D = x.shape
    tile = lambda n: pl.BlockSpec((1, tm, n), lambda b, i: (b, i, 0))
    const = lambda a: pl.BlockSpec(a.shape, lambda b, i: (0, 0))
    return pl.pallas_call(
        _merge_kernel,
        grid=(B, S // tm),
        in_specs=[tile(LRU_WIDTH), tile(RW_WIDTH), tile(ATT_WIDTH), tile(3 * D), tile(D),
                  pl.BlockSpec((1, 1, D), lambda b, i: (b, 0, 0)),
                  const(wa), const(wb), const(wc), const(wo)],
        out_specs=tile(D),
        out_shape=jax.ShapeDtypeStruct((B, S, D), F32),
        compiler_params=_params(("arbitrary", "arbitrary")),
        name="merge",
    )(ya, yb, yc, gates, x, g1, wa, wb, wc, wo)


FFN_CHUNKS = ((0, 1024), (1024, 1024), (2048, 768))


def _ffn_kernel(x_ref, sc_ref, sh_ref, g2_ref, wi_ref, wo_ref, o_ref):
    x = x_ref[0]
    ms = jnp.mean(x * x, axis=-1, keepdims=True)
    h = (x * lax.rsqrt(ms + RMS_EPS) * (1.0 + sc_ref[0]) + sh_ref[0]).astype(BF16)
    acc = jnp.zeros(x.shape, F32)
    for off, width in FFN_CHUNKS:
        gate = jnp.dot(h, wi_ref[:, off:off + width], preferred_element_type=F32)
        up = jnp.dot(h, wi_ref[:, D_FF + off:D_FF + off + width], preferred_element_type=F32)
        act = (gate * _sigmoid(gate) * up).astype(BF16)
        acc = acc + jnp.dot(act, wo_ref[off:off + width, :], preferred_element_type=F32)
    o_ref[0] = x + g2_ref[0] * acc


def _ffn(x, sc, sh, g2, wi, wo, tm):
    B, S, D = x.shape
    tile = pl.BlockSpec((1, tm, D), lambda b, i: (b, i, 0))
    vec = pl.BlockSpec((1, 1, D), lambda b, i: (b, 0, 0))
    return pl.pallas_call(
        _ffn_kernel,
        grid=(B, S // tm),
        in_specs=[tile, vec, vec, vec,
                  pl.BlockSpec(wi.shape, lambda b, i: (0, 0)),
                  pl.BlockSpec(wo.shape, lambda b, i: (0, 0))],
        out_specs=tile,
        out_shape=jax.ShapeDtypeStruct((B, S, D), F32),
        compiler_params=_params(("arbitrary", "arbitrary")),
        name="ffn",
    )(x, sc, sh, g2, wi, wo)


def _pad_rows(w, rows):
    return jnp.pad(w, ((0, rows - w.shape[0]), (0, 0)))


def _pad_cols(w, cols):
    return jnp.pad(w, ((0, 0), (0, cols - w.shape[1])))


def _block_diag_gates(wa, wx):
    per = LRU_SUPER // LRU_BLOCK
    n_super = LRU_WIDTH // LRU_SUPER
    on_diag = jnp.eye(per, dtype=bool)[None, :, None, :, None]

    def block_diag(w):
        w = w.reshape(n_super, per, LRU_BLOCK, 1, LRU_BLOCK)
        return jnp.where(on_diag, w, 0.0).reshape(n_super, LRU_SUPER, LRU_SUPER)

    return jnp.concatenate([block_diag(wa), block_diag(wx)], axis=2).astype(BF16)


def _tile_rows(S, want):
    t = min(want, S)
    while S % t:
        t //= 2
    return t


def kernel(x, c, w_ada, b_ada, w_in, conv_w, conv_b, lru_wa, lru_ba, lru_wx, lru_bx, lru_lambda, w_lru_o, rw_mu, rw_w0, rw_w2, rw_a0, rw_a2, rw_g2, rw_kk, rw_ka, rw_rk, rw_ln_w, rw_ln_b, rw_v0, rw_v1, rw_v2, w_rw_o, q_gain, k_gain, sinks, w_att_o, w_out, w_ffn_in, w_ffn_out):
    B, S, D = x.shape
    depth = w_in.shape[0]
    assert D == D_MODEL and S % (ATT_BLOCKS_PER_STEP * WINDOW) == 0
    tm_mm = _tile_rows(S, 512)
    tm_seq = _tile_rows(S, 256)
    ts_scan = _tile_rows(S, RW_GROUPS_PER_STEP * RW_GROUP * RW_CHUNK)

    mod = _ada(c, w_ada, b_ada).reshape(depth, B, 6, 1, D)

    o_lru = 2 * LRU_WIDTH
    o_rw = o_lru + 3 * RW_WIDTH + RW_DECAY_LORA + RW_AAA_LORA + RW_GATE_LORA
    o_att = o_rw + ATT_WIDTH + 2 * KV_WIDTH

    v_first = None
    for i in range(depth):
        sh1, sc1, g1, sh2, sc2, g2 = (mod[i, :, j] for j in range(6))
        wi = w_in[i]
        w_lru = wi[:, :o_lru].astype(BF16)
        w_rw_cols = [_pad_cols(wi[:, o_lru:o_rw], RW_COLS_PAD)]
        if i > 0:
            w_rw_cols.append(_pad_cols(rw_v1[i - 1], RW_MV_PAD))
        w_rw = jnp.concatenate(w_rw_cols, axis=1).astype(BF16)
        w_att = wi[:, o_rw:o_att].astype(BF16)
        w_gate = wi[:, o_att:].astype(BF16)

        h = _norm_mod(x, sc1, sh1, tm_mm)
        p_att = _mm(h, w_att, tm_mm, F32, "proj_att")
        p_gate = _mm(h, w_gate, tm_mm, F32, "proj_gate")

        ya = _lru_branch(h, w_lru, conv_w[i], conv_b[i], _block_diag_gates(lru_wa[i], lru_wx[i]),
                         lru_ba[i], lru_bx[i], lru_lambda[i], tm_seq)

        mu = _pad_cols(rw_mu[i].reshape(1, -1), RW_COLS_PAD)
        w2p = _pad_rows(rw_w2[i], LANES).astype(BF16)
        a2p = jnp.concatenate([jnp.zeros_like(rw_a2[i]), rw_a2[i]], axis=0).astype(BF16)
        g2p = _pad_rows(rw_g2[i], RW_GATE_PAD).astype(BF16)
        mix = None
        if i > 0:
            mix = (v_first, rw_v0[i - 1].reshape(1, -1), _pad_rows(rw_v2[i - 1], RW_MV_PAD).astype(BF16))
        r, lw, k, v, a, g = _rw_prep(h, w_rw, mu, rw_w0[i].reshape(1, -1), rw_a0[i].reshape(1, -1),
                                     w2p, a2p, g2p, tm_seq, mix)
        if i == 0:
            v_first = v
        yb = _rw_scan(r, lw, k, v, a, g, rw_kk[i].reshape(1, -1), rw_ka[i].reshape(1, -1),
                      rw_rk[i].reshape(1, -1), rw_ln_w[i].reshape(1, -1), rw_ln_b[i].reshape(1, -1), ts_scan)

        sinks_b = jnp.broadcast_to(sinks[i].reshape(-1, 1), (ATT_HEADS, LANES))
        yc = _attention(p_att, q_gain[i], k_gain[i], sinks_b)

        x = _merge(ya, yb, yc, p_gate, x, g1, w_lru_o[i].astype(BF16), w_rw_o[i].astype(BF16),
                   w_att_o[i].astype(BF16), w_out[i].astype(BF16), tm_mm)
        x = _ffn(x, sc2, sh2, g2, w_ffn_in[i].astype(BF16), w_ffn_out[i].astype(BF16), tm_mm)
    return x
```
